```python
import jax, jax.numpy as jnp
from jax import lax
import numpy as np

D_MODEL = 2048
BATCH = 1
SEQ = 16384
DEPTH = 1
DEC_BATCH = 4
DEC_SEQ = 4096
PAST_LEN = 128

N_META = 16
GRID_W = 64
HEAD_DIM = 128
ATTN_HEADS = 8
KV_HEADS = 2
GROUP = ATTN_HEADS // KV_HEADS
ATTN_DIM = ATTN_HEADS * HEAD_DIM
KV_DIM = KV_HEADS * HEAD_DIM
HG_HEADS = 8
HG_DIM = HG_HEADS * HEAD_DIM
MIX_DIM = ATTN_DIM + HG_DIM
IN_DIM = ATTN_DIM + 2 * KV_DIM + 5 * HG_DIM
SPLITS = (ATTN_DIM,
          ATTN_DIM + KV_DIM,
          ATTN_DIM + 2 * KV_DIM,
          ATTN_DIM + 2 * KV_DIM + HG_DIM,
          ATTN_DIM + 2 * KV_DIM + 2 * HG_DIM,
          ATTN_DIM + 2 * KV_DIM + 3 * HG_DIM,
          ATTN_DIM + 2 * KV_DIM + 4 * HG_DIM)
D_FF = 5632
CONV_W = 3
Q_BLOCK = 128
CHUNK = 16
ROPE_THETA = 10000.0
EPS = 1e-6

kernel_name = 'hymba_hgrn2_axial_gqa_encoder'


def _rmsnorm(x, w):
    xf = x.astype(jnp.float32)
    y = xf * lax.rsqrt(jnp.mean(xf * xf, axis=-1, keepdims=True) + EPS)
    return (y * w.astype(jnp.float32)).astype(x.dtype)


def _axial_rope_tables(n_tokens):
    rows = n_tokens // GRID_W
    zeros = jnp.zeros((N_META,), jnp.float32)
    row_ids = jnp.concatenate([zeros, jnp.repeat(jnp.arange(rows, dtype=jnp.float32), GRID_W)])
    col_ids = jnp.concatenate([zeros, jnp.tile(jnp.arange(GRID_W, dtype=jnp.float32), rows)])
    n_freq = HEAD_DIM // 4
    inv_freq = jnp.power(ROPE_THETA, -jnp.arange(n_freq, dtype=jnp.float32) / n_freq)
    ang_r = row_ids[:, None] * inv_freq
    ang_c = col_ids[:, None] * inv_freq
    return (jnp.cos(ang_r), jnp.sin(ang_r), jnp.cos(ang_c), jnp.sin(ang_c))


def _rotate(x, cos, sin):
    a, b = jnp.split(x, 2, axis=-1)
    return jnp.concatenate([a * cos - b * sin, b * cos + a * sin], axis=-1)


def _apply_axial_rope(x, tables):
    cr, sr, cc, sc = tables
    xf = x.astype(jnp.float32)
    half = HEAD_DIM // 2
    out = jnp.concatenate([_rotate(xf[..., :half], cr, sr), _rotate(xf[..., half:], cc, sc)], axis=-1)
    return out.astype(x.dtype)


def _attend_block(qb, k, v):
    s = jnp.einsum('bkgqd,bksd->bkgqs', qb, k, preferred_element_type=jnp.float32) * (HEAD_DIM ** -0.5)
    p = jax.nn.softmax(s, axis=-1)
    return jnp.einsum('bkgqs,bksd->bkgqd', p.astype(v.dtype), v)


def _bidir_gqa(q, k, v):
    B, _, L, _ = q.shape
    n_real = L - N_META
    nb = n_real // Q_BLOCK
    q = q.reshape(B, KV_HEADS, GROUP, L, HEAD_DIM)
    o_meta = _attend_block(q[:, :, :, :N_META], k, v)
    q_real = q[:, :, :, N_META:].reshape(B, KV_HEADS, GROUP, nb, Q_BLOCK, HEAD_DIM)
    q_real = jnp.moveaxis(q_real, 3, 0)
    o_real = lax.map(lambda qb: _attend_block(qb, k, v), q_real)
    o_real = jnp.moveaxis(o_real, 0, 3).reshape(B, KV_HEADS, GROUP, n_real, HEAD_DIM)
    o = jnp.concatenate([o_meta, o_real], axis=3)
    return o.reshape(B, ATTN_HEADS, L, HEAD_DIM)


def _chunk_gla(q, k, v, g):
    B, H, L, dk = q.shape
    dv = v.shape[-1]
    nc = L // CHUNK
    q, k, v, g = (t.reshape(B, H, nc, CHUNK, t.shape[-1]) for t in (q, k, v, g))
    b = jnp.cumsum(g, axis=3)
    b_last = b[:, :, :, -1:, :]
    q_i = q * jnp.exp(b) * (dk ** -0.5)
    k_i = k * jnp.exp(-b)
    k_e = k * jnp.exp(b_last - b)
    lower = jnp.tril(jnp.ones((CHUNK, CHUNK), dtype=bool))
    a = jnp.where(lower, jnp.einsum('bhnid,bhnjd->bhnij', q_i, k_i), 0.0)
    o_intra = jnp.einsum('bhnij,bhnjv->bhniv', a, v)
    decay = jnp.exp(b_last[:, :, :, 0, :])

    def step(S, xs):
        qc, kc, vc, dc = xs
        o = jnp.einsum('bhid,bhdv->bhiv', qc, S)
        S = dc[..., :, None] * S + jnp.einsum('bhid,bhiv->bhdv', kc, vc)
        return S, o

    xs = tuple(jnp.moveaxis(t, 2, 0) for t in (q_i, k_e, v, decay))
    _, o_inter = lax.scan(step, jnp.zeros((B, H, dk, dv), q.dtype), xs)
    o = o_intra + jnp.moveaxis(o_inter, 0, 2)
    return o.reshape(B, H, L, dv)


def _hgrn2_gates(z, lb):
    lb = lb.reshape(HG_HEADS, 1, HEAD_DIM)
    f = lb + (1.0 - lb) * jax.nn.sigmoid(z.astype(jnp.float32))
    return 1.0 - f, jnp.log(f)


def _mixer(h, w_in, w_out, q_norm_w, k_norm_w, hg_norm_w, lb_f, lb_b, tables):
    B, L, _ = h.shape
    proj = h @ w_in
    aq, ak, av, hq, hff, hfb, hi, hg = jnp.split(proj, list(SPLITS), axis=-1)

    def heads(t, n):
        return t.reshape(B, L, n, HEAD_DIM).transpose(0, 2, 1, 3)

    aq = _apply_axial_rope(_rmsnorm(heads(aq, ATTN_HEADS), q_norm_w), tables)
    ak = _apply_axial_rope(_rmsnorm(heads(ak, KV_HEADS), k_norm_w), tables)
    av = heads(av, KV_HEADS)
    attn = _bidir_gqa(aq, ak, av).transpose(0, 2, 1, 3).reshape(B, L, ATTN_DIM)

    qf = heads(hq, HG_HEADS).astype(jnp.float32)
    vf = heads(hi, HG_HEADS).astype(jnp.float32)
    k_f, g_f = _hgrn2_gates(heads(hff, HG_HEADS), lb_f)
    k_b, g_b = _hgrn2_gates(heads(hfb, HG_HEADS), lb_b)
    o_f = _chunk_gla(qf, k_f, vf, g_f)
    flip = lambda t: jnp.flip(t, axis=2)
    o_b = flip(_chunk_gla(flip(qf), flip(k_b), flip(vf), flip(g_b)))
    o = (o_f + o_b).transpose(0, 2, 1, 3)
    o = _rmsnorm(o, hg_norm_w.reshape(HG_HEADS, HEAD_DIM)).reshape(B, L, HG_DIM)
    hgrn = (o * jax.nn.silu(hg.astype(jnp.float32))).astype(h.dtype)

    return jnp.concatenate([attn, hgrn], axis=-1) @ w_out


def _dwconv3(u, w, b):
    up = jnp.pad(u, ((0, 0), (1, 1), (0, 0)))
    return up[:, :-2] * w[0] + up[:, 1:-1] * w[1] + up[:, 2:] * w[2] + b


def _conv_ffn(h, w_up, conv_w, conv_b, w_down):
    u = _dwconv3(h @ w_up, conv_w, conv_b)
    val, gate = jnp.split(u, 2, axis=-1)
    return (jax.nn.silu(gate) * val) @ w_down


def _trunk(x, meta_tokens, w_in, w_out, norm_mix_w, q_norm_w, k_norm_w, hg_norm_w,
           lb_fwd, lb_bwd, norm_ffn_w, w_up, conv_w, conv_b, w_down):
    B, n_tok, _ = x.shape
    tables = _axial_rope_tables(n_tok)
    meta = jnp.broadcast_to(meta_tokens.astype(x.dtype)[None], (B, N_META, D_MODEL))
    h = jnp.concatenate([meta, x], axis=1)
    lbf_all = jnp.cumsum(jax.nn.softmax(lb_fwd.astype(jnp.float32), axis=0), axis=0)
    lbb_all = jnp.cumsum(jax.nn.softmax(lb_bwd.astype(jnp.float32), axis=0), axis=0)
    for l in range(DEPTH):
        h = h + _mixer(_rmsnorm(h, norm_mix_w[l]), w_in[l], w_out[l], q_norm_w[l], k_norm_w[l],
                       hg_norm_w[l], lbf_all[l], lbb_all[l], tables)
        h = h + _conv_ffn(_rmsnorm(h, norm_ffn_w[l]), w_up[l], conv_w[l], conv_b[l], w_down[l])
    return h[:, N_META:]


def setup_inputs(seed: int = 0) -> dict:
    key = jax.random.key(seed)
    ks = jax.random.split(key, 16)
    f32 = jnp.float32
    nrm = lambda k, shape, s: jax.random.normal(k, shape, f32) * s
    return {
        'x_prompt': nrm(ks[0], (BATCH, SEQ, D_MODEL), 1.0),
        'x_sample': nrm(ks[1], (DEC_BATCH, DEC_SEQ, D_MODEL), 1.0),
        'meta_tokens': nrm(ks[2], (N_META, D_MODEL), 1.0),
        'w_in': nrm(ks[3], (DEPTH, D_MODEL, IN_DIM), D_MODEL ** -0.5),
        'w_out': nrm(ks[4], (DEPTH, MIX_DIM, D_MODEL), MIX_DIM ** -0.5),
        'norm_mix_w': 1.0 + nrm(ks[5], (DEPTH, D_MODEL), 0.02),
        'q_norm_w': 1.0 + nrm(ks[6], (DEPTH, HEAD_DIM), 0.02),
        'k_norm_w': 1.0 + nrm(ks[7], (DEPTH, HEAD_DIM), 0.02),
        'hg_norm_w': 1.0 + nrm(ks[8], (DEPTH, HG_DIM), 0.02),
        'lb_fwd': nrm(ks[9], (DEPTH + 1, HG_DIM), 0.1),
        'lb_bwd': nrm(ks[10], (DEPTH + 1, HG_DIM), 0.1),
        'norm_ffn_w': 1.0 + nrm(ks[11], (DEPTH, D_MODEL), 0.02),
        'w_up': nrm(ks[12], (DEPTH, D_MODEL, 2 * D_FF), D_MODEL ** -0.5),
        'conv_w': nrm(ks[13], (DEPTH, CONV_W, 2 * D_FF), CONV_W ** -0.5),
        'conv_b': nrm(ks[14], (DEPTH, 2 * D_FF), 0.02),
        'w_down': nrm(ks[15], (DEPTH, D_FF, D_MODEL), D_FF ** -0.5),
    }


def reference(x_prompt, x_sample, meta_tokens, w_in, w_out, norm_mix_w, q_norm_w, k_norm_w,
              hg_norm_w, lb_fwd, lb_bwd, norm_ffn_w, w_up, conv_w, conv_b, w_down):
    y_prompt = _trunk(x_prompt, meta_tokens, w_in, w_out, norm_mix_w, q_norm_w, k_norm_w, hg_norm_w,
                      lb_fwd, lb_bwd, norm_ffn_w, w_up, conv_w, conv_b, w_down)
    y_sample = _trunk(x_sample, meta_tokens, w_in, w_out, norm_mix_w, q_norm_w, k_norm_w, hg_norm_w,
                      lb_fwd, lb_bwd, norm_ffn_w, w_up, conv_w, conv_b, w_down)
    return (y_prompt, y_sample)
```

```python
import functools
import math

import jax
import jax.numpy as jnp
from jax import lax
from jax.experimental import pallas as pl
from jax.experimental.pallas import tpu as pltpu

D_MODEL = 2048
N_META = 16
GRID_W = 64
HEAD_DIM = 128
ATTN_HEADS = 8
KV_HEADS = 2
GROUP = ATTN_HEADS // KV_HEADS
ATTN_DIM = ATTN_HEADS * HEAD_DIM
KV_DIM = KV_HEADS * HEAD_DIM
HG_HEADS = 8
HG_DIM = HG_HEADS * HEAD_DIM
HP_DIM = 5 * HG_DIM
IN_DIM = ATTN_DIM + 2 * KV_DIM + HP_DIM
D_FF = 5632
ROPE_THETA = 10000.0
EPS = 1e-6

META_PAD = 128
VT_CHUNK = 512
NEG_BIG = -1e30
VMEM_LIMIT = 56 * 1024 * 1024

F32 = jnp.float32
BF16 = jnp.bfloat16


def _cparams(n_axes):
    return pltpu.CompilerParams(
        dimension_semantics=("arbitrary",) * n_axes, vmem_limit_bytes=VMEM_LIMIT)


def _rms(x, w):
    return x * lax.rsqrt(jnp.mean(x * x, axis=-1, keepdims=True) + EPS) * w


IN_CHUNK = 512
Q_SCALE = HEAD_DIM ** -0.5 * math.log2(math.e)


def _norm_rope(t, w, cos, sin):
    y = _rms(t, w)
    lane = lax.broadcasted_iota(jnp.int32, y.shape, 1)
    partner = jnp.where((lane % 64) < 32,
                        pltpu.roll(y, HEAD_DIM - 32, 1), pltpu.roll(y, 32, 1))
    return y * cos + partner * sin


def _inproj_kernel(x_ref, nw_ref, w_ref, cos_ref, sin_ref, qw_ref, kw_ref,
                   q_ref, k_ref, vt_ref, hp_ref):
    hn = _rms(x_ref[...], nw_ref[...]).astype(BF16)
    cos = cos_ref[...]
    sin = sin_ref[...]
    n_chunks = IN_DIM // IN_CHUNK
    for c in range(n_chunks):
        acc = jnp.dot(hn, w_ref[:, c * IN_CHUNK:(c + 1) * IN_CHUNK],
                      preferred_element_type=F32)
        if c < 2:
            for h in range(4):
                t = _norm_rope(acc[:, h * 128:(h + 1) * 128], qw_ref[...], cos, sin)
                col = (c * 4 + h) * 128
                q_ref[:, col:col + 128] = (t * Q_SCALE).astype(BF16)
        elif c == 2:
            for h in range(KV_HEADS):
                t = _norm_rope(acc[:, h * 128:(h + 1) * 128], kw_ref[...], cos, sin)
                k_ref[:, h * 128:(h + 1) * 128] = t.astype(BF16)
            for h in range(KV_HEADS):
                v = acc[:, KV_DIM + h * 128:KV_DIM + (h + 1) * 128]
                vt_ref[h, 0] = v.T.astype(BF16)
        else:
            col = (c - 3) * IN_CHUNK
            hp_ref[:, col:col + IN_CHUNK] = acc.astype(BF16)


def _inproj(x, norm_w, w_in, cos_t, sin_t, q_w, k_w, tm, seq_len):
    T = x.shape[0]
    n_tab = seq_len // tm
    grid = (T // tm,)
    full = lambda i: (0, 0)
    return pl.pallas_call(
        _inproj_kernel,
        grid=grid,
        in_specs=[
            pl.BlockSpec((tm, D_MODEL), lambda i: (i, 0)),
            pl.BlockSpec((1, D_MODEL), full),
            pl.BlockSpec((D_MODEL, IN_DIM), full, pipeline_mode=pl.Buffered(1)),
            pl.BlockSpec((tm, HEAD_DIM), lambda i: (i % n_tab, 0)),
            pl.BlockSpec((tm, HEAD_DIM), lambda i: (i % n_tab, 0)),
            pl.BlockSpec((1, HEAD_DIM), full),
            pl.BlockSpec((1, HEAD_DIM), full),
        ],
        out_specs=[
            pl.BlockSpec((tm, ATTN_DIM), lambda i: (i, 0)),
            pl.BlockSpec((tm, KV_DIM), lambda i: (i, 0)),
            pl.BlockSpec((KV_HEADS, 1, HEAD_DIM, tm), lambda i: (0, i, 0, 0)),
            pl.BlockSpec((tm, HP_DIM), lambda i: (i, 0)),
        ],
        out_shape=[
            jax.ShapeDtypeStruct((T, ATTN_DIM), BF16),
            jax.ShapeDtypeStruct((T, KV_DIM), BF16),
            jax.ShapeDtypeStruct((KV_HEADS, T // tm, HEAD_DIM, tm), BF16),
            jax.ShapeDtypeStruct((T, HP_DIM), BF16),
        ],
        compiler_params=_cparams(1),
        name="inproj",
    )(x, norm_w, w_in, cos_t, sin_t, q_w, k_w)


def _attn_kernel(q_ref, k_ref, vt_ref, km_ref, vtm_ref, o_ref, qt_sc, acc_sc, *, tq, n_chunks):
    mq = GROUP * tq
    qt = q_ref[...].astype(F32).T
    for h in range(GROUP):
        qt_sc[:, h * tq:(h + 1) * tq] = qt[h * 128:(h + 1) * 128, :].astype(BF16)
    qt_all = qt_sc[...]

    s = jnp.dot(km_ref[...], qt_all, preferred_element_type=F32)
    row = lax.broadcasted_iota(jnp.int32, s.shape, 0)
    s = jnp.where(row < N_META, s, NEG_BIG)
    m0 = jnp.max(s, axis=0, keepdims=True)
    p = jnp.exp2(s - m0)
    l0 = jnp.sum(p, axis=0, keepdims=True)
    acc_sc[...] = jnp.dot(vtm_ref[...], p.astype(BF16), preferred_element_type=F32)

    def body(j, carry):
        m, l = carry
        start = pl.multiple_of(j * VT_CHUNK, VT_CHUNK)
        s = jnp.dot(k_ref[pl.ds(start, VT_CHUNK), :], qt_all, preferred_element_type=F32)
        m_new = jnp.maximum(m, jnp.max(s, axis=0, keepdims=True))
        alpha = jnp.exp2(m - m_new)
        p = jnp.exp2(s - m_new)
        l_new = alpha * l + jnp.sum(p, axis=0, keepdims=True)
        pv = jnp.dot(vt_ref[j], p.astype(BF16), preferred_element_type=F32)
        acc_sc[...] = alpha * acc_sc[...] + pv
        return m_new, l_new

    _, l = lax.fori_loop(0, n_chunks, body, (m0, l0))
    o_t = acc_sc[...] * (1.0 / l)
    for h in range(GROUP):
        o_ref[:, h * 128:(h + 1) * 128] = o_t[:, h * tq:(h + 1) * tq].T.astype(BF16)


def _attention(q, k, vt, km, vtm, B, n, tq):
    Bq, nq, _ = q.shape
    n_chunks = n // VT_CHUNK
    k3 = k.reshape(B, n, KV_DIM)
    qb = (lambda b: b) if Bq > 1 else (lambda b: 0)
    return pl.pallas_call(
        functools.partial(_attn_kernel, tq=tq, n_chunks=n_chunks),
        grid=(B, KV_HEADS, nq // tq),
        in_specs=[
            pl.BlockSpec((None, tq, GROUP * HEAD_DIM), lambda b, g, i: (qb(b), i, g)),
            pl.BlockSpec((None, n, HEAD_DIM), lambda b, g, i: (b, 0, g)),
            pl.BlockSpec((None, n_chunks, HEAD_DIM, VT_CHUNK), lambda b, g, i: (g, b, 0, 0)),
            pl.BlockSpec((META_PAD, HEAD_DIM), lambda b, g, i: (0, g)),
            pl.BlockSpec((None, None, HEAD_DIM, META_PAD), lambda b, g, i: (g, 0, 0, 0)),
        ],
        out_specs=pl.BlockSpec((None, tq, GROUP * HEAD_DIM), lambda b, g, i: (b, i, g)),
        out_shape=jax.ShapeDtypeStruct((B, nq, ATTN_DIM), BF16),
        scratch_shapes=[
            pltpu.VMEM((HEAD_DIM, GROUP * tq), BF16),
            pltpu.VMEM((HEAD_DIM, GROUP * tq), F32),
        ],
        compiler_params=_cparams(3),
        name="attention",
    )(q, k3, vt, km, vtm)


HG_CHUNK = 64
HG_BLOCK = 512
HP_Q, HP_FF, HP_FB, HP_I, HP_G = 0, 8, 16, 24, 32


def _layer0_lower_bound(lb_ref):
    lb = lb_ref[...]
    e = jnp.exp(lb - jnp.max(lb, axis=0, keepdims=True))
    return e[0:1, :] / jnp.sum(e, axis=0, keepdims=True)


def _gla_chunk(q, z, v, lb, s_t, reverse, n_valid=None):
    c = q.shape[0]
    f = lb + (1.0 - lb) * jax.nn.sigmoid(z)
    kk = 1.0 - f
    g = jnp.log(f)
    if n_valid is not None:
        valid = lax.broadcasted_iota(jnp.int32, g.shape, 0) < n_valid
        g = jnp.where(valid, g, 0.0)
        kk = jnp.where(valid, kk, 0.0)
    ri = lax.broadcasted_iota(jnp.int32, (c, c), 0)
    ci = lax.broadcasted_iota(jnp.int32, (c, c), 1)
    keep = (ri <= ci) if reverse else (ri >= ci)
    tri = jnp.where(keep, 1.0, 0.0).astype(BF16)
    g_hi = g.astype(BF16)
    g_lo = (g - g_hi.astype(F32)).astype(BF16)
    b = (jnp.dot(tri, g_hi, preferred_element_type=F32)
         + jnp.dot(tri, g_lo, preferred_element_type=F32))
    b_end = b[0:1, :] if reverse else b[c - 1:c, :]
    r = b[c // 2:c // 2 + 1, :]
    e1 = jnp.exp(b - r)
    e2 = jnp.exp(r - b)
    q_t = q * (HEAD_DIM ** -0.5) * e1
    k_t = kk * e2
    a = lax.dot_general(q_t.astype(BF16), k_t.astype(BF16), (((1,), (1,)), ((), ())),
                        preferred_element_type=F32)
    a = jnp.where(keep, a, 0.0)
    vb = v.astype(BF16)
    o = jnp.dot(a.astype(BF16), vb, preferred_element_type=F32)
    q_e = q_t * jnp.exp(r)
    k_e = k_t * jnp.exp(b_end - r)
    o = o + lax.dot_general(q_e.astype(BF16), s_t.astype(BF16), (((1,), (1,)), ((), ())),
                            preferred_element_type=F32)
    s_new = s_t * jnp.exp(b_end) + lax.dot_general(
        vb, k_e.astype(BF16), (((0,), (0,)), ((), ())), preferred_element_type=F32)
    return o, s_new


def _hgrn_fwd_kernel(q_ref, z_ref, v_ref, qm_ref, zm_ref, vm_ref, lb_ref,
                     o_ref, om_ref, s_sc, *, n_sub):
    i = pl.program_id(2)
    lb = _layer0_lower_bound(lb_ref)
    C = HG_CHUNK

    @pl.when(i == 0)
    def _():
        o_m, s1 = _gla_chunk(qm_ref[0:C, :].astype(F32), zm_ref[0:C, :].astype(F32),
                             vm_ref[0:C, :].astype(F32), lb,
                             jnp.zeros((HEAD_DIM, HEAD_DIM), F32), False, n_valid=N_META)
        om_ref[...] = o_m
        s_sc[...] = s1

    s_t = s_sc[...]
    for c in range(n_sub):
        sl = slice(c * C, (c + 1) * C)
        o, s_t = _gla_chunk(q_ref[sl, :].astype(F32), z_ref[sl, :].astype(F32),
                            v_ref[sl, :].astype(F32), lb, s_t, False)
        o_ref[sl, :] = o
    s_sc[...] = s_t


def _hgrn_bwd_kernel(q_ref, z_ref, v_ref, gate_ref, of_ref,
                     qm_ref, zm_ref, vm_ref, gatem_ref, ofm_ref, lb_ref, nw_ref,
                     out_ref, outm_ref, s_sc, *, n_sub):
    i = pl.program_id(2)
    last = pl.num_programs(2) - 1
    lb = _layer0_lower_bound(lb_ref)
    nw = nw_ref[...]
    C = HG_CHUNK

    def finish(o_f, o_b, gate):
        y = _rms(o_f + o_b, nw)
        return (y * jax.nn.silu(gate.astype(F32))).astype(BF16)

    @pl.when(i == 0)
    def _():
        s_sc[...] = jnp.zeros_like(s_sc)

    s_t = s_sc[...]
    for c in reversed(range(n_sub)):
        sl = slice(c * C, (c + 1) * C)
        o, s_t = _gla_chunk(q_ref[sl, :].astype(F32), z_ref[sl, :].astype(F32),
                            v_ref[sl, :].astype(F32), lb, s_t, True)
        out_ref[sl, :] = finish(of_ref[sl, :], o, gate_ref[sl, :])
    s_sc[...] = s_t

    @pl.when(i == last)
    def _():
        o_m, _ = _gla_chunk(qm_ref[0:C, :].astype(F32), zm_ref[0:C, :].astype(F32),
                            vm_ref[0:C, :].astype(F32), lb, s_t, True, n_valid=N_META)
        outm_ref[...] = finish(ofm_ref[...], o_m, gatem_ref[0:C, :])


def _hgrn(hp, hpm, lb_fwd, lb_bwd, hg_norm_w, B, n):
    tb = HG_BLOCK
    nb = n // tb
    n_sub = tb // HG_CHUNK
    C = HG_CHUNK
    grid = (B, HG_HEADS, nb)

    def tok(off, rev):
        if rev:
            return pl.BlockSpec((tb, HEAD_DIM), lambda b, h, i: (b * nb + nb - 1 - i, off + h))
        return pl.BlockSpec((tb, HEAD_DIM), lambda b, h, i: (b * nb + i, off + h))

    def met(off):
        return pl.BlockSpec((META_PAD, HEAD_DIM), lambda b, h, i: (0, off + h))

    lb_spec = pl.BlockSpec((lb_fwd.shape[0], HEAD_DIM), lambda b, h, i: (0, h))
    o_f, o_fm = pl.pallas_call(
        functools.partial(_hgrn_fwd_kernel, n_sub=n_sub),
        grid=grid,
        in_specs=[tok(HP_Q, False), tok(HP_FF, False), tok(HP_I, False),
                  met(HP_Q), met(HP_FF), met(HP_I), lb_spec],
        out_specs=[
            pl.BlockSpec((tb, HEAD_DIM), lambda b, h, i: (b * nb + i, h)),
            pl.BlockSpec((None, C, HEAD_DIM), lambda b, h, i: (b, 0, h)),
        ],
        out_shape=[
            jax.ShapeDtypeStruct((B * n, HG_DIM), F32),
            jax.ShapeDtypeStruct((B, C, HG_DIM), F32),
        ],
        scratch_shapes=[pltpu.VMEM((HEAD_DIM, HEAD_DIM), F32)],
        compiler_params=_cparams(3),
        name="hgrn_fwd",
    )(hp, hp, hp, hpm, hpm, hpm, lb_fwd)

    out, out_m = pl.pallas_call(
        functools.partial(_hgrn_bwd_kernel, n_sub=n_sub),
        grid=grid,
        in_specs=[tok(HP_Q, True), tok(HP_FB, True), tok(HP_I, True), tok(HP_G, True),
                  pl.BlockSpec((tb, HEAD_DIM), lambda b, h, i: (b * nb + nb - 1 - i, h)),
                  met(HP_Q), met(HP_FB), met(HP_I), met(HP_G),
                  pl.BlockSpec((None, C, HEAD_DIM), lambda b, h, i: (b, 0, h)),
                  lb_spec,
                  pl.BlockSpec((1, HEAD_DIM), lambda b, h, i: (0, h))],
        out_specs=[
            pl.BlockSpec((tb, HEAD_DIM), lambda b, h, i: (b * nb + nb - 1 - i, h)),
            pl.BlockSpec((None, C, HEAD_DIM), lambda b, h, i: (b, 0, h)),
        ],
        out_shape=[
            jax.ShapeDtypeStruct((B * n, HG_DIM), BF16),
            jax.ShapeDtypeStruct((B, C, HG_DIM), BF16),
        ],
        scratch_shapes=[pltpu.VMEM((HEAD_DIM, HEAD_DIM), F32)],
        compiler_params=_cparams(3),
        name="hgrn_bwd",
    )(hp, hp, hp, hp, o_f, hpm, hpm, hpm, hpm, o_fm, lb_bwd, hg_norm_w)
    return out, out_m


def _outproj_kernel(x_ref, a_ref, g_ref, wa_ref, wg_ref, o_ref):
    o_ref[...] = (x_ref[...]
                  + jnp.dot(a_ref[...], wa_ref[...], preferred_element_type=F32)
                  + jnp.dot(g_ref[...], wg_ref[...], preferred_element_type=F32))


def _outproj(x, attn, hg, w_out, tm):
    T = x.shape[0]
    return pl.pallas_call(
        _outproj_kernel,
        grid=(T // tm,),
        in_specs=[
            pl.BlockSpec((tm, D_MODEL), lambda i: (i, 0)),
            pl.BlockSpec((tm, ATTN_DIM), lambda i: (i, 0)),
            pl.BlockSpec((tm, HG_DIM), lambda i: (i, 0)),
            pl.BlockSpec((ATTN_DIM, D_MODEL), lambda i: (0, 0), pipeline_mode=pl.Buffered(1)),
            pl.BlockSpec((HG_DIM, D_MODEL), lambda i: (1, 0), pipeline_mode=pl.Buffered(1)),
        ],
        out_specs=pl.BlockSpec((tm, D_MODEL), lambda i: (i, 0)),
        out_shape=jax.ShapeDtypeStruct((T, D_MODEL), F32),
        compiler_params=_cparams(1),
        name="outproj",
    )(x, attn, hg, w_out, w_out)


FFN_TF = 512
HALO = 16


def _ffn_kernel(h_ref, hprev_ref, hnext_ref, hmeta_ref, nw_ref, wv_ref, wg_ref,
                cwv_ref, cwg_ref, cbv_ref, cbg_ref, wd_ref, o_ref, hn_sc, *, tm):
    i = pl.program_id(1)
    f = pl.program_id(2)
    last_i = pl.num_programs(1) - 1

    @pl.when(f == 0)
    def _():
        nw = nw_ref[...]
        h = h_ref[...]
        prev = jnp.where(i == 0, hmeta_ref[...], hprev_ref[...])
        hn_sc[0:HALO, :] = _rms(prev, nw).astype(BF16)
        hn_sc[HALO:HALO + tm, :] = _rms(h, nw).astype(BF16)
        nxt = jnp.where(i == last_i, 0.0, _rms(hnext_ref[...], nw))
        hn_sc[HALO + tm:2 * HALO + tm, :] = nxt.astype(BF16)
        o_ref[...] = h

    hn = hn_sc[...]

    def conv(w_ref, cw_ref, cb_ref):
        u = jnp.dot(hn, w_ref[...], preferred_element_type=F32)
        cw = cw_ref[...]
        return (u[HALO - 1:HALO - 1 + tm] * cw[0:1] + u[HALO:HALO + tm] * cw[1:2]
                + u[HALO + 1:HALO + 1 + tm] * cw[2:3] + cb_ref[...])

    val = conv(wv_ref, cwv_ref, cbv_ref)
    gate = conv(wg_ref, cwg_ref, cbg_ref)
    act = (jax.nn.silu(gate) * val).astype(BF16)
    o_ref[...] += jnp.dot(act, wd_ref[...], preferred_element_type=F32)


def _ffn(h1, h1m, norm_w, w_up, conv_w, conv_b, w_down, B, n, tm):
    tf = FFN_TF
    nf = D_FF // tf
    ni = n // tm
    rb = tm // HALO
    n_halo = n // HALO
    return pl.pallas_call(
        functools.partial(_ffn_kernel, tm=tm),
        grid=(B, ni, nf),
        in_specs=[
            pl.BlockSpec((None, tm, D_MODEL), lambda b, i, f: (b, i, 0)),
            pl.BlockSpec((None, HALO, D_MODEL), lambda b, i, f: (b, jnp.maximum(i * rb - 1, 0), 0)),
            pl.BlockSpec((None, HALO, D_MODEL),
                         lambda b, i, f: (b, jnp.minimum((i + 1) * rb, n_halo - 1), 0)),
            pl.BlockSpec((None, N_META, D_MODEL), lambda b, i, f: (b, 0, 0)),
            pl.BlockSpec((1, D_MODEL), lambda b, i, f: (0, 0)),
            pl.BlockSpec((D_MODEL, tf), lambda b, i, f: (0, f)),
            pl.BlockSpec((D_MODEL, tf), lambda b, i, f: (0, nf + f)),
            pl.BlockSpec((3, tf), lambda b, i, f: (0, f)),
            pl.BlockSpec((3, tf), lambda b, i, f: (0, nf + f)),
            pl.BlockSpec((1, tf), lambda b, i, f: (0, f)),
            pl.BlockSpec((1, tf), lambda b, i, f: (0, nf + f)),
            pl.BlockSpec((tf, D_MODEL), lambda b, i, f: (f, 0)),
        ],
        out_specs=pl.BlockSpec((None, tm, D_MODEL), lambda b, i, f: (b, i, 0)),
        out_shape=jax.ShapeDtypeStruct((B, n, D_MODEL), F32),
        scratch_shapes=[pltpu.VMEM((tm + 2 * HALO, D_MODEL), BF16)],
        compiler_params=_cparams(3),
        name="ffn",
    )(h1, h1, h1, h1m, norm_w, w_up, w_up, conv_w, conv_w, conv_b, conv_b, w_down)


def _rope_tables(n):
    pos = jnp.arange(n, dtype=jnp.int32)
    n_freq = HEAD_DIM // 4
    inv_freq = jnp.power(ROPE_THETA, -jnp.arange(n_freq, dtype=F32) / n_freq)
    ang_r = (pos // GRID_W).astype(F32)[:, None] * inv_freq
    ang_c = (pos % GRID_W).astype(F32)[:, None] * inv_freq
    cos = jnp.concatenate([jnp.cos(ang_r)] * 2 + [jnp.cos(ang_c)] * 2, axis=-1)
    sin = jnp.concatenate([-jnp.sin(ang_r), jnp.sin(ang_r),
                           -jnp.sin(ang_c), jnp.sin(ang_c)], axis=-1)
    return cos, sin


def _trunk(x, meta, mp, wts, cos_t, sin_t, tiles):
    (w_in, w_out, norm_mix_w, q_w, k_w, hg_w, lb_fwd, lb_bwd,
     norm_ffn_w, w_up, conv_w, conv_b, w_down) = wts
    q_m, k_m, vt_m, hp_m = mp
    B, n, _ = x.shape
    T = B * n
    xf = x.reshape(T, D_MODEL)

    q, k, vt, hp = _inproj(xf, norm_mix_w, w_in, cos_t, sin_t, q_w, k_w, VT_CHUNK, n)
    attn = _attention(q.reshape(B, n, ATTN_DIM), k, vt, k_m, vt_m, B, n, tiles["tq"])
    attn_m = _attention(q_m.reshape(1, META_PAD, ATTN_DIM), k, vt, k_m, vt_m, B, n, META_PAD)
    hg, hg_m = _hgrn(hp, hp_m, lb_fwd, lb_bwd, hg_w, B, n)

    h1 = _outproj(xf, attn.reshape(T, ATTN_DIM), hg, w_out, tiles["tm_out"])
    x_m = jnp.broadcast_to(meta[None], (B, N_META, D_MODEL)).reshape(B * N_META, D_MODEL)
    h1_m = _outproj(x_m, attn_m[:, :N_META].reshape(B * N_META, ATTN_DIM),
                    hg_m[:, :N_META].reshape(B * N_META, HG_DIM), w_out, B * N_META)

    return _ffn(h1.reshape(B, n, D_MODEL), h1_m.reshape(B, N_META, D_MODEL),
                norm_ffn_w, w_up, conv_w, conv_b, w_down, B, n, tiles["tm_ffn"])


def _forward(x_prompt, x_sample, meta_tokens, w_in, w_out, norm_mix_w, q_norm_w, k_norm_w,
             hg_norm_w, lb_fwd, lb_bwd, norm_ffn_w, w_up, conv_w, conv_b, w_down, tiles):
    wts = (w_in[0].astype(BF16), w_out[0].astype(BF16), norm_mix_w[0][None], q_norm_w[0][None],
           k_norm_w[0][None], hg_norm_w[0][None], lb_fwd, lb_bwd, norm_ffn_w[0][None],
           w_up[0].astype(BF16), conv_w[0], conv_b[0][None], w_down[0].astype(BF16))
    n_max = max(x_prompt.shape[1], x_sample.shape[1])
    cos_t, sin_t = _rope_tables(n_max)

    meta_pad = jnp.zeros((META_PAD, D_MODEL), F32).at[:N_META].set(meta_tokens)
    mp = _inproj(meta_pad, wts[2], wts[0], jnp.ones((META_PAD, HEAD_DIM), F32),
                 jnp.zeros((META_PAD, HEAD_DIM), F32), wts[3], wts[4], META_PAD, META_PAD)

    y_p = _trunk(x_prompt, meta_tokens, mp, wts, cos_t, sin_t, tiles)
    y_s = _trunk(x_sample, meta_tokens, mp, wts, cos_t, sin_t, tiles)
    return (y_p, y_s)


TILES = {"tq": 256, "tm_out": 512, "tm_ffn": 512}


def kernel(x_prompt, x_sample, meta_tokens, w_in, w_out, norm_mix_w, q_norm_w, k_norm_w,
           hg_norm_w, lb_fwd, lb_bwd, norm_ffn_w, w_up, conv_w, conv_b, w_down):
    return _forward(x_prompt, x_sample, meta_tokens, w_in, w_out, norm_mix_w, q_norm_w,
                    k_norm_w, hg_norm_w, lb_fwd, lb_bwd, norm_ffn_w, w_up, conv_w, conv_b,
                    w_down, TILES)
```

```python
import functools
import math

import jax
import jax.numpy as jnp
from jax import lax
from jax.experimental import pallas as pl
from jax.experimental.pallas import tpu as pltpu

D_MODEL = 2048
N_META = 16
GRID_W = 64
HEAD_DIM = 128
ATTN_HEADS = 8
KV_HEADS = 2
GROUP = ATTN_HEADS // KV_HEADS
ATTN_DIM = ATTN_HEADS * HEAD_DIM
KV_DIM = KV_HEADS * HEAD_DIM
HG_HEADS = 8
HG_DIM = HG_HEADS * HEAD_DIM
HP_DIM = 5 * HG_DIM
IN_DIM = ATTN_DIM + 2 * KV_DIM + HP_DIM
D_FF = 5632
ROPE_THETA = 10000.0
EPS = 1e-6

META_PAD = 128
VT_CHUNK = 512
NEG_BIG = -1e30
VMEM_LIMIT = 56 * 1024 * 1024

F32 = jnp.float32
BF16 = jnp.bfloat16


def _cparams(n_axes):
    return pltpu.CompilerParams(
        dimension_semantics=("arbitrary",) * n_axes, vmem_limit_bytes=VMEM_LIMIT)


def _rms(x, w):
    return x * lax.rsqrt(jnp.mean(x * x, axis=-1, keepdims=True) + EPS) * w


IN_CHUNK = 512
Q_SCALE = HEAD_DIM ** -0.5 * math.log2(math.e)


def _norm_rope(t, w, cos, sin):
    y = _rms(t, w)
    lane = lax.broadcasted_iota(jnp.int32, y.shape, 1)
    partner = jnp.where((lane % 64) < 32,
                        pltpu.roll(y, HEAD_DIM - 32, 1), pltpu.roll(y, 32, 1))
    return y * cos + partner * sin


def _inproj_kernel(x_ref, nw_ref, w_ref, cos_ref, sin_ref, qw_ref, kw_ref,
                   q_ref, k_ref, vt_ref, hp_ref):
    hn = _rms(x_ref[...], nw_ref[...]).astype(BF16)
    cos = cos_ref[...]
    sin = sin_ref[...]
    n_chunks = IN_DIM // IN_CHUNK
    for c in range(n_chunks):
        acc = jnp.dot(hn, w_ref[:, c * IN_CHUNK:(c + 1) * IN_CHUNK],
                      preferred_element_type=F32)
        if c < 2:
            for h in range(4):
                t = _norm_rope(acc[:, h * 128:(h + 1) * 128], qw_ref[...], cos, sin)
                col = (c * 4 + h) * 128
                q_ref[:, col:col + 128] = (t * Q_SCALE).astype(BF16)
        elif c == 2:
            for h in range(KV_HEADS):
                t = _norm_rope(acc[:, h * 128:(h + 1) * 128], kw_ref[...], cos, sin)
                k_ref[:, h * 128:(h + 1) * 128] = t.astype(BF16)
            for h in range(KV_HEADS):
                v = acc[:, KV_DIM + h * 128:KV_DIM + (h + 1) * 128]
                vt_ref[h, 0] = v.T.astype(BF16)
        else:
            col = (c - 3) * IN_CHUNK
            hp_ref[:, col:col + IN_CHUNK] = acc.astype(BF16)


def _inproj(x, norm_w, w_in, cos_t, sin_t, q_w, k_w, tm, seq_len):
    T = x.shape[0]
    n_tab = seq_len // tm
    grid = (T // tm,)
    full = lambda i: (0, 0)
    return pl.pallas_call(
        _inproj_kernel,
        grid=grid,
        in_specs=[
            pl.BlockSpec((tm, D_MODEL), lambda i: (i, 0)),
            pl.BlockSpec((1, D_MODEL), full),
            pl.BlockSpec((D_MODEL, IN_DIM), full, pipeline_mode=pl.Buffered(1)),
            pl.BlockSpec((tm, HEAD_DIM), lambda i: (i % n_tab, 0)),
            pl.BlockSpec((tm, HEAD_DIM), lambda i: (i % n_tab, 0)),
            pl.BlockSpec((1, HEAD_DIM), full),
            pl.BlockSpec((1, HEAD_DIM), full),
        ],
        out_specs=[
            pl.BlockSpec((tm, ATTN_DIM), lambda i: (i, 0)),
            pl.BlockSpec((tm, KV_DIM), lambda i: (i, 0)),
            pl.BlockSpec((KV_HEADS, 1, HEAD_DIM, tm), lambda i: (0, i, 0, 0)),
            pl.BlockSpec((tm, HP_DIM), lambda i: (i, 0)),
        ],
        out_shape=[
            jax.ShapeDtypeStruct((T, ATTN_DIM), BF16),
            jax.ShapeDtypeStruct((T, KV_DIM), BF16),
            jax.ShapeDtypeStruct((KV_HEADS, T // tm, HEAD_DIM, tm), BF16),
            jax.ShapeDtypeStruct((T, HP_DIM), BF16),
        ],
        compiler_params=_cparams(1),
        name="inproj",
    )(x, norm_w, w_in, cos_t, sin_t, q_w, k_w)


def _attn_kernel(q_ref, k_ref, vt_ref, km_ref, vtm_ref, o_ref, qt_sc, s0_sc, s1_sc, acc_sc,
                 *, tq, n_chunks):
    heads = range(GROUP)
    cols = [slice(h * tq, (h + 1) * tq) for h in heads]
    qt = q_ref[...].astype(F32).T
    for h in heads:
        qt_sc[:, cols[h]] = qt[h * 128:(h + 1) * 128, :].astype(BF16)

    def scores(j, s_sc):
        start = pl.multiple_of(j * VT_CHUNK, VT_CHUNK)
        kc = k_ref[pl.ds(start, VT_CHUNK), :]
        cm = []
        for h in heads:
            s = jnp.dot(kc, qt_sc[:, cols[h]], preferred_element_type=F32)
            s_sc[:, cols[h]] = s
            cm.append(jnp.max(s, axis=0, keepdims=True))
        return tuple(cm)

    def absorb(j, s_sc, m, l, cm):
        vt = vt_ref[j]
        m_out, l_out = [], []
        for h in heads:
            m_new = jnp.maximum(m[h], cm[h])
            alpha = jnp.exp2(m[h] - m_new)
            p = jnp.exp2(s_sc[:, cols[h]] - m_new)
            l_out.append(alpha * l[h] + jnp.sum(p, axis=0, keepdims=True))
            pv = jnp.dot(vt, p.astype(BF16), preferred_element_type=F32)
            acc_sc[:, cols[h]] = alpha * acc_sc[:, cols[h]] + pv
            m_out.append(m_new)
        return tuple(m_out), tuple(l_out)

    m0, l0 = [], []
    for h in heads:
        s = jnp.dot(km_ref[...], qt_sc[:, cols[h]], preferred_element_type=F32)
        row = lax.broadcasted_iota(jnp.int32, s.shape, 0)
        s = jnp.where(row < N_META, s, NEG_BIG)
        mh = jnp.max(s, axis=0, keepdims=True)
        p = jnp.exp2(s - mh)
        m0.append(mh)
        l0.append(jnp.sum(p, axis=0, keepdims=True))
        acc_sc[:, cols[h]] = jnp.dot(vtm_ref[...], p.astype(BF16), preferred_element_type=F32)
    cm0 = scores(0, s0_sc)

    def pair(i, carry):
        m, l, cm = carry
        cm1 = scores(2 * i + 1, s1_sc)
        m, l = absorb(2 * i, s0_sc, m, l, cm)
        cm2 = scores(2 * i + 2, s0_sc)
        m, l = absorb(2 * i + 1, s1_sc, m, l, cm1)
        return m, l, cm2

    m, l, cm = lax.fori_loop(0, n_chunks // 2 - 1, pair, (tuple(m0), tuple(l0), cm0))
    cm1 = scores(n_chunks - 1, s1_sc)
    m, l = absorb(n_chunks - 2, s0_sc, m, l, cm)
    m, l = absorb(n_chunks - 1, s1_sc, m, l, cm1)

    for h in heads:
        o_t = acc_sc[:, cols[h]] * (1.0 / l[h])
        o_ref[:, h * 128:(h + 1) * 128] = o_t.T.astype(BF16)


def _attention(q, k, vt, km, vtm, B, n, tq):
    Bq, nq, _ = q.shape
    n_chunks = n // VT_CHUNK
    assert n_chunks >= 2 and n_chunks % 2 == 0
    k3 = k.reshape(B, n, KV_DIM)
    qb = (lambda b: b) if Bq > 1 else (lambda b: 0)
    return pl.pallas_call(
        functools.partial(_attn_kernel, tq=tq, n_chunks=n_chunks),
        grid=(B, KV_HEADS, nq // tq),
        in_specs=[
            pl.BlockSpec((None, tq, GROUP * HEAD_DIM), lambda b, g, i: (qb(b), i, g)),
            pl.BlockSpec((None, n, HEAD_DIM), lambda b, g, i: (b, 0, g)),
            pl.BlockSpec((None, n_chunks, HEAD_DIM, VT_CHUNK), lambda b, g, i: (g, b, 0, 0)),
            pl.BlockSpec((META_PAD, HEAD_DIM), lambda b, g, i: (0, g)),
            pl.BlockSpec((None, None, HEAD_DIM, META_PAD), lambda b, g, i: (g, 0, 0, 0)),
        ],
        out_specs=pl.BlockSpec((None, tq, GROUP * HEAD_DIM), lambda b, g, i: (b, i, g)),
        out_shape=jax.ShapeDtypeStruct((B, nq, ATTN_DIM), BF16),
        scratch_shapes=[
            pltpu.VMEM((HEAD_DIM, GROUP * tq), BF16),
            pltpu.VMEM((VT_CHUNK, GROUP * tq), F32),
            pltpu.VMEM((VT_CHUNK, GROUP * tq), F32),
            pltpu.VMEM((HEAD_DIM, GROUP * tq), F32),
        ],
        compiler_params=_cparams(3),
        name="attention",
    )(q, k3, vt, km, vtm)


HG_CHUNK = 64
HG_GROUP = 128
HG_BLOCK = 512
HG_HPS = 4
HP_Q, HP_FF, HP_FB, HP_I, HP_G = 0, 8, 16, 24, 32


def _layer0_lower_bound(lb_ref):
    lb = lb_ref[...]
    e = jnp.exp(lb - jnp.max(lb, axis=0, keepdims=True))
    return e[0:1, :] / jnp.sum(e, axis=0, keepdims=True)


def _gla_block(q, z, v, lb, s, reverse, n_valid=None):
    tb = q.shape[0]
    C, G = HG_CHUNK, HG_GROUP
    nc, ng = tb // C, tb // G
    tn = (((0,), (0,)), ((), ()))

    f = lb + (1.0 - lb) * jax.nn.sigmoid(z)
    kk = 1.0 - f
    g = jnp.log(f)
    if n_valid is not None:
        valid = lax.broadcasted_iota(jnp.int32, g.shape, 0) < n_valid
        g = jnp.where(valid, g, 0.0)
        kk = jnp.where(valid, kk, 0.0)

    ri = lax.broadcasted_iota(jnp.int32, (G, G), 0)
    ci = lax.broadcasted_iota(jnp.int32, (G, G), 1)
    causal = (ri <= ci) if reverse else (ri >= ci)
    keep = jnp.logical_and(ri // C == ci // C, causal)
    tri = jnp.where(keep, 1.0, 0.0).astype(BF16)

    g_hi = g.astype(BF16)
    g_lo = (g - g_hi.astype(F32)).astype(BF16)
    ghl = jnp.concatenate([g_hi, g_lo], axis=1)
    b_parts = []
    for gi in range(ng):
        part = jnp.dot(tri, ghl[gi * G:(gi + 1) * G], preferred_element_type=F32)
        b_parts.append(part[:, :HEAD_DIM] + part[:, HEAD_DIM:])
    b = jnp.concatenate(b_parts, axis=0) if ng > 1 else b_parts[0]

    def chunk_rows(idx):
        return [b[c * C + idx:c * C + idx + 1, :] for c in range(nc)]

    def spread(rows):
        return jnp.concatenate([jnp.broadcast_to(x, (C, HEAD_DIM)) for x in rows], axis=0)

    r_rows = chunk_rows(C // 2)
    end_rows = chunk_rows(0 if reverse else C - 1)
    r_all = spread(r_rows)
    q_t = q * (HEAD_DIM ** -0.5) * jnp.exp(b - r_all)
    k_t = kk * jnp.exp(r_all - b)
    q_e = (q_t * spread([jnp.exp(r) for r in r_rows])).astype(BF16)
    k_e = (k_t * spread([jnp.exp(e - r) for e, r in zip(end_rows, r_rows)])).astype(BF16)
    q_tb = q_t.astype(BF16)
    k_tb = k_t.astype(BF16)
    vb = v.astype(BF16)

    o_intra = []
    for gi in range(ng):
        sl = slice(gi * G, (gi + 1) * G)
        a = jnp.dot(q_tb[sl], k_t[sl].T.astype(BF16), preferred_element_type=F32)
        a = jnp.where(keep, a, 0.0).astype(BF16)
        o_intra.append(jnp.dot(a, vb[sl], preferred_element_type=F32))

    ends = jnp.concatenate(end_rows + [jnp.zeros((HEAD_DIM - nc, HEAD_DIM), F32)], axis=0)
    decay_t = jnp.exp(ends).T

    o_inter = [None] * nc
    for c in (reversed(range(nc)) if reverse else range(nc)):
        sl = slice(c * C, (c + 1) * C)
        o_inter[c] = jnp.dot(q_e[sl], s.astype(BF16), preferred_element_type=F32)
        kv = lax.dot_general(k_e[sl], vb[sl], tn, preferred_element_type=F32)
        s = s * decay_t[:, c:c + 1] + kv
    o = jnp.concatenate(o_intra, axis=0) + jnp.concatenate(o_inter, axis=0)
    return o, s


def _head_cols(h):
    return slice(h * HEAD_DIM, (h + 1) * HEAD_DIM)


def _hgrn_fwd_kernel(q_ref, z_ref, v_ref, qm_ref, zm_ref, vm_ref, lb_ref,
                     o_ref, om_ref, s_sc):
    i = pl.program_id(2)
    lb = _layer0_lower_bound(lb_ref)

    @pl.when(i == 0)
    def _():
        for h in range(HG_HPS):
            cs = _head_cols(h)
            o_m, s1 = _gla_block(qm_ref[:, cs].astype(F32), zm_ref[:, cs].astype(F32),
                                 vm_ref[:, cs].astype(F32), lb[:, cs],
                                 jnp.zeros((HEAD_DIM, HEAD_DIM), F32), False, n_valid=N_META)
            om_ref[:, cs] = o_m
            s_sc[h] = s1

    for h in range(HG_HPS):
        cs = _head_cols(h)
        o, s_t = _gla_block(q_ref[:, cs].astype(F32), z_ref[:, cs].astype(F32),
                            v_ref[:, cs].astype(F32), lb[:, cs], s_sc[h], False)
        o_ref[:, cs] = o
        s_sc[h] = s_t


def _hgrn_bwd_kernel(q_ref, z_ref, v_ref, gate_ref, of_ref,
                     qm_ref, zm_ref, vm_ref, gatem_ref, ofm_ref, lb_ref, nw_ref,
                     out_ref, outm_ref, s_sc):
    i = pl.program_id(2)
    last = pl.num_programs(2) - 1
    lb = _layer0_lower_bound(lb_ref)
    nw = nw_ref[...]

    def finish(o_f, o_b, gate, w):
        return (_rms(o_f + o_b, w) * jax.nn.silu(gate.astype(F32))).astype(BF16)

    @pl.when(i == 0)
    def _():
        s_sc[...] = jnp.zeros_like(s_sc)

    for h in range(HG_HPS):
        cs = _head_cols(h)
        o, s_t = _gla_block(q_ref[:, cs].astype(F32), z_ref[:, cs].astype(F32),
                            v_ref[:, cs].astype(F32), lb[:, cs], s_sc[h], True)
        out_ref[:, cs] = finish(of_ref[:, cs], o, gate_ref[:, cs], nw[:, cs])
        s_sc[h] = s_t

    @pl.when(i == last)
    def _():
        for h in range(HG_HPS):
            cs = _head_cols(h)
            o_m, _ = _gla_block(qm_ref[:, cs].astype(F32), zm_ref[:, cs].astype(F32),
                                vm_ref[:, cs].astype(F32), lb[:, cs], s_sc[h], True,
                                n_valid=N_META)
            outm_ref[:, cs] = finish(ofm_ref[:, cs], o_m, gatem_ref[:, cs], nw[:, cs])


def _hgrn(hp, hpm, lb_fwd, lb_bwd, hg_norm_w, B, n):
    tb = HG_BLOCK
    nb = n // tb
    W = HG_HPS * HEAD_DIM
    grid = (B, HG_HEADS // HG_HPS, nb)

    def tok(off, rev):
        cb = off // HG_HPS
        if rev:
            return pl.BlockSpec((tb, W), lambda b, h, i: (b * nb + nb - 1 - i, cb + h))
        return pl.BlockSpec((tb, W), lambda b, h, i: (b * nb + i, cb + h))

    def met(off):
        cb = off // HG_HPS
        return pl.BlockSpec((META_PAD, W), lambda b, h, i: (0, cb + h))

    lb_spec = pl.BlockSpec((lb_fwd.shape[0], W), lambda b, h, i: (0, h))
    meta_out = pl.BlockSpec((None, META_PAD, W), lambda b, h, i: (b, 0, h))
    state = pltpu.VMEM((HG_HPS, HEAD_DIM, HEAD_DIM), F32)
    o_f, o_fm = pl.pallas_call(
        _hgrn_fwd_kernel,
        grid=grid,
        in_specs=[tok(HP_Q, False), tok(HP_FF, False), tok(HP_I, False),
                  met(HP_Q), met(HP_FF), met(HP_I), lb_spec],
        out_specs=[pl.BlockSpec((tb, W), lambda b, h, i: (b * nb + i, h)), meta_out],
        out_shape=[
            jax.ShapeDtypeStruct((B * n, HG_DIM), F32),
            jax.ShapeDtypeStruct((B, META_PAD, HG_DIM), F32),
        ],
        scratch_shapes=[state],
        compiler_params=_cparams(3),
        name="hgrn_fwd",
    )(hp, hp, hp, hpm, hpm, hpm, lb_fwd)

    out, out_m = pl.pallas_call(
        _hgrn_bwd_kernel,
        grid=grid,
        in_specs=[tok(HP_Q, True), tok(HP_FB, True), tok(HP_I, True), tok(HP_G, True),
                  pl.BlockSpec((tb, W), lambda b, h, i: (b * nb + nb - 1 - i, h)),
                  met(HP_Q), met(HP_FB), met(HP_I), met(HP_G), meta_out,
                  lb_spec,
                  pl.BlockSpec((1, W), lambda b, h, i: (0, h))],
        out_specs=[pl.BlockSpec((tb, W), lambda b, h, i: (b * nb + nb - 1 - i, h)), meta_out],
        out_shape=[
            jax.ShapeDtypeStruct((B * n, HG_DIM), BF16),
            jax.ShapeDtypeStruct((B, META_PAD, HG_DIM), BF16),
        ],
        scratch_shapes=[state],
        compiler_params=_cparams(3),
        name="hgrn_bwd",
    )(hp, hp, hp, hp, o_f, hpm, hpm, hpm, hpm, o_fm, lb_bwd, hg_norm_w)
    return out, out_m


def _outproj_kernel(x_ref, a_ref, g_ref, wa_ref, wg_ref, o_ref):
    o_ref[...] = (x_ref[...]
                  + jnp.dot(a_ref[...], wa_ref[...], preferred_element_type=F32)
                  + jnp.dot(g_ref[...], wg_ref[...], preferred_element_type=F32))


def _outproj(x, attn, hg, w_out, tm):
    T = x.shape[0]
    return pl.pallas_call(
        _outproj_kernel,
        grid=(T // tm,),
        in_specs=[
            pl.BlockSpec((tm, D_MODEL), lambda i: (i, 0)),
            pl.BlockSpec((tm, ATTN_DIM), lambda i: (i, 0)),
            pl.BlockSpec((tm, HG_DIM), lambda i: (i, 0)),
            pl.BlockSpec((ATTN_DIM, D_MODEL), lambda i: (0, 0), pipeline_mode=pl.Buffered(1)),
            pl.BlockSpec((HG_DIM, D_MODEL), lambda i: (1, 0), pipeline_mode=pl.Buffered(1)),
        ],
        out_specs=pl.BlockSpec((tm, D_MODEL), lambda i: (i, 0)),
        out_shape=jax.ShapeDtypeStruct((T, D_MODEL), F32),
        compiler_params=_cparams(1),
        name="outproj",
    )(x, attn, hg, w_out, w_out)


FFN_TF = 512
HALO = 16


def _ffn_kernel(h_ref, hprev_ref, hnext_ref, hmeta_ref, nw_ref, wv_ref, wg_ref,
                cwv_ref, cwg_ref, cbv_ref, cbg_ref, wd_ref, o_ref, hn_sc, *, tm):
    i = pl.program_id(1)
    f = pl.program_id(2)
    last_i = pl.num_programs(1) - 1

    @pl.when(f == 0)
    def _():
        nw = nw_ref[...]
        h = h_ref[...]
        prev = jnp.where(i == 0, hmeta_ref[...], hprev_ref[...])
        hn_sc[0:HALO, :] = _rms(prev, nw).astype(BF16)
        hn_sc[HALO:HALO + tm, :] = _rms(h, nw).astype(BF16)
        nxt = jnp.where(i == last_i, 0.0, _rms(hnext_ref[...], nw))
        hn_sc[HALO + tm:2 * HALO + tm, :] = nxt.astype(BF16)
        o_ref[...] = h

    hn = hn_sc[...]
    half = wv_ref.shape[1] // 2

    def conv(w_ref, cw_ref, cb_ref, cs):
        u = jnp.dot(hn, w_ref[:, cs], preferred_element_type=F32)
        cw = cw_ref[:, cs]
        return (u[HALO - 1:HALO - 1 + tm] * cw[0:1] + u[HALO:HALO + tm] * cw[1:2]
                + u[HALO + 1:HALO + 1 + tm] * cw[2:3] + cb_ref[:, cs])

    down = []
    for c in range(2):
        cs = slice(c * half, (c + 1) * half)
        val = conv(wv_ref, cwv_ref, cbv_ref, cs)
        gate = conv(wg_ref, cwg_ref, cbg_ref, cs)
        act = (jax.nn.silu(gate) * val).astype(BF16)
        down.append(jnp.dot(act, wd_ref[cs, :], preferred_element_type=F32))
    o_ref[...] += down[0] + down[1]


def _ffn(h1, h1m, norm_w, w_up, conv_w, conv_b, w_down, B, n, tm):
    tf = FFN_TF
    nf = D_FF // tf
    ni = n // tm
    rb = tm // HALO
    n_halo = n // HALO
    return pl.pallas_call(
        functools.partial(_ffn_kernel, tm=tm),
        grid=(B, ni, nf),
        in_specs=[
            pl.BlockSpec((None, tm, D_MODEL), lambda b, i, f: (b, i, 0)),
            pl.BlockSpec((None, HALO, D_MODEL), lambda b, i, f: (b, jnp.maximum(i * rb - 1, 0), 0)),
            pl.BlockSpec((None, HALO, D_MODEL),
                         lambda b, i, f: (b, jnp.minimum((i + 1) * rb, n_halo - 1), 0)),
            pl.BlockSpec((None, N_META, D_MODEL), lambda b, i, f: (b, 0, 0)),
            pl.BlockSpec((1, D_MODEL), lambda b, i, f: (0, 0)),
            pl.BlockSpec((D_MODEL, tf), lambda b, i, f: (0, f)),
            pl.BlockSpec((D_MODEL, tf), lambda b, i, f: (0, nf + f)),
            pl.BlockSpec((3, tf), lambda b, i, f: (0, f)),
            pl.BlockSpec((3, tf), lambda b, i, f: (0, nf + f)),
            pl.BlockSpec((1, tf), lambda b, i, f: (0, f)),
            pl.BlockSpec((1, tf), lambda b, i, f: (0, nf + f)),
            pl.BlockSpec((tf, D_MODEL), lambda b, i, f: (f, 0)),
        ],
        out_specs=pl.BlockSpec((None, tm, D_MODEL), lambda b, i, f: (b, i, 0)),
        out_shape=jax.ShapeDtypeStruct((B, n, D_MODEL), F32),
        scratch_shapes=[pltpu.VMEM((tm + 2 * HALO, D_MODEL), BF16)],
        compiler_params=_cparams(3),
        name="ffn",
    )(h1, h1, h1, h1m, norm_w, w_up, w_up, conv_w, conv_w, conv_b, conv_b, w_down)


def _rope_tables(n):
    pos = jnp.arange(n, dtype=jnp.int32)
    n_freq = HEAD_DIM // 4
    inv_freq = jnp.power(ROPE_THETA, -jnp.arange(n_freq, dtype=F32) / n_freq)
    ang_r = (pos // GRID_W).astype(F32)[:, None] * inv_freq
    ang_c = (pos % GRID_W).astype(F32)[:, None] * inv_freq
    cos = jnp.concatenate([jnp.cos(ang_r)] * 2 + [jnp.cos(ang_c)] * 2, axis=-1)
    sin = jnp.concatenate([-jnp.sin(ang_r), jnp.sin(ang_r),
                           -jnp.sin(ang_c), jnp.sin(ang_c)], axis=-1)
    return cos, sin


def _trunk(x, meta, mp, wts, cos_t, sin_t, tiles):
    (w_in, w_out, norm_mix_w, q_w, k_w, hg_w, lb_fwd, lb_bwd,
     norm_ffn_w, w_up, conv_w, conv_b, w_down) = wts
    q_m, k_m, vt_m, hp_m = mp
    B, n, _ = x.shape
    T = B * n
    xf = x.reshape(T, D_MODEL)

    q, k, vt, hp = _inproj(xf, norm_mix_w, w_in, cos_t, sin_t, q_w, k_w, VT_CHUNK, n)
    attn = _attention(q.reshape(B, n, ATTN_DIM), k, vt, k_m, vt_m, B, n, tiles["tq"])
    attn_m = _attention(q_m.reshape(1, META_PAD, ATTN_DIM), k, vt, k_m, vt_m, B, n, META_PAD)
    hg, hg_m = _hgrn(hp, hp_m, lb_fwd, lb_bwd, hg_w, B, n)

    h1 = _outproj(xf, attn.reshape(T, ATTN_DIM), hg, w_out, tiles["tm_out"])
    x_m = jnp.broadcast_to(meta[None], (B, N_META, D_MODEL)).reshape(B * N_META, D_MODEL)
    h1_m = _outproj(x_m, attn_m[:, :N_META].reshape(B * N_META, ATTN_DIM),
                    hg_m[:, :N_META].reshape(B * N_META, HG_DIM), w_out, B * N_META)

    return _ffn(h1.reshape(B, n, D_MODEL), h1_m.reshape(B, N_META, D_MODEL),
                norm_ffn_w, w_up, conv_w, conv_b, w_down, B, n, tiles["tm_ffn"])


def _forward(x_prompt, x_sample, meta_tokens, w_in, w_out, norm_mix_w, q_norm_w, k_norm_w,
             hg_norm_w, lb_fwd, lb_bwd, norm_ffn_w, w_up, conv_w, conv_b, w_down, tiles):
    wts = (w_in[0].astype(BF16), w_out[0].astype(BF16), norm_mix_w[0][None], q_norm_w[0][None],
           k_norm_w[0][None], hg_norm_w[0][None], lb_fwd, lb_bwd, norm_ffn_w[0][None],
           w_up[0].astype(BF16), conv_w[0], conv_b[0][None], w_down[0].astype(BF16))
    n_max = max(x_prompt.shape[1], x_sample.shape[1])
    cos_t, sin_t = _rope_tables(n_max)

    meta_pad = jnp.zeros((META_PAD, D_MODEL), F32).at[:N_META].set(meta_tokens)
    mp = _inproj(meta_pad, wts[2], wts[0], jnp.ones((META_PAD, HEAD_DIM), F32),
                 jnp.zeros((META_PAD, HEAD_DIM), F32), wts[3], wts[4], META_PAD, META_PAD)

    y_p = _trunk(x_prompt, meta_tokens, mp, wts, cos_t, sin_t, tiles)
    y_s = _trunk(x_sample, meta_tokens, mp, wts, cos_t, sin_t, tiles)
    return (y_p, y_s)


TILES = {"tq": 512, "tm_out": 512, "tm_ffn": 512}


def kernel(x_prompt, x_sample, meta_tokens, w_in, w_out, norm_mix_w, q_norm_w, k_norm_w,
           hg_norm_w, lb_fwd, lb_bwd, norm_ffn_w, w_up, conv_w, conv_b, w_down):
    return _forward(x_prompt, x_sample, meta_tokens, w_in, w_out, norm_mix_w, q_norm_w,
                    k_norm_w, hg_norm_w, lb_fwd, lb_bwd, norm_ffn_w, w_up, conv_w, conv_b,
                    w_down, TILES)
```

```python
import functools
import math

import jax
import jax.numpy as jnp
from jax import lax
from jax.experimental import pallas as pl
from jax.experimental.pallas import tpu as pltpu

D_MODEL = 2048
N_META = 16
GRID_W = 64
HEAD_DIM = 128
ATTN_HEADS = 8
KV_HEADS = 2
GROUP = ATTN_HEADS // KV_HEADS
ATTN_DIM = ATTN_HEADS * HEAD_DIM
KV_DIM = KV_HEADS * HEAD_DIM
HG_HEADS = 8
HG_DIM = HG_HEADS * HEAD_DIM
HP_DIM = 5 * HG_DIM
IN_DIM = ATTN_DIM + 2 * KV_DIM + HP_DIM
D_FF = 5632
ROPE_THETA = 10000.0
EPS = 1e-6

META_PAD = 128
VT_CHUNK = 512
ATT_CHUNK = 512
NEG_BIG = -1e30
VMEM_LIMIT = 56 * 1024 * 1024

F32 = jnp.float32
BF16 = jnp.bfloat16


def _cparams(n_axes):
    return pltpu.CompilerParams(
        dimension_semantics=("arbitrary",) * n_axes, vmem_limit_bytes=VMEM_LIMIT)


def _rms(x, w):
    return x * lax.rsqrt(jnp.mean(x * x, axis=-1, keepdims=True) + EPS) * w


IN_CHUNK = 512
Q_SCALE = HEAD_DIM ** -0.5 * math.log2(math.e)


def _norm_rope(t, w, cos, sin):
    y = _rms(t, w)
    lane = lax.broadcasted_iota(jnp.int32, y.shape, 1)
    partner = jnp.where((lane % 64) < 32,
                        pltpu.roll(y, HEAD_DIM - 32, 1), pltpu.roll(y, 32, 1))
    return y * cos + partner * sin


def _token_table(row_ref, col_ref):
    col = col_ref[...]
    return jnp.concatenate(
        [jnp.broadcast_to(row_ref[g:g + 1, :], col.shape) + col for g in range(row_ref.shape[0])],
        axis=0)


def _inproj_kernel(x_ref, nw_ref, w_ref, cosr_ref, sinr_ref, cosc_ref, sinc_ref, qw_ref, kw_ref,
                   q_ref, k_ref, vt_ref, hp_ref):
    hn = _rms(x_ref[...], nw_ref[...]).astype(BF16)
    cos = _token_table(cosr_ref, cosc_ref)
    sin = _token_table(sinr_ref, sinc_ref)
    n_chunks = IN_DIM // IN_CHUNK
    for c in range(n_chunks):
        acc = jnp.dot(hn, w_ref[:, c * IN_CHUNK:(c + 1) * IN_CHUNK],
                      preferred_element_type=F32)
        if c < 2:
            for h in range(4):
                t = _norm_rope(acc[:, h * 128:(h + 1) * 128], qw_ref[...], cos, sin)
                col = (c * 4 + h) * 128
                q_ref[:, col:col + 128] = (t * Q_SCALE).astype(BF16)
        elif c == 2:
            for h in range(KV_HEADS):
                t = _norm_rope(acc[:, h * 128:(h + 1) * 128], kw_ref[...], cos, sin)
                k_ref[:, h * 128:(h + 1) * 128] = t.astype(BF16)
            for h in range(KV_HEADS):
                v = acc[:, KV_DIM + h * 128:KV_DIM + (h + 1) * 128]
                vt_ref[h, 0] = v.T.astype(BF16)
        else:
            col = (c - 3) * IN_CHUNK
            hp_ref[:, col:col + IN_CHUNK] = acc.astype(BF16)


def _inproj(x, norm_w, w_in, tables, q_w, k_w, tm, seq_len):
    T = x.shape[0]
    n_tab = seq_len // tm
    tr = tm // GRID_W
    grid = (T // tm,)
    full = lambda i: (0, 0)
    row_spec = pl.BlockSpec((tr, HEAD_DIM), lambda i: (i % n_tab, 0))
    col_spec = pl.BlockSpec((GRID_W, HEAD_DIM), full)
    return pl.pallas_call(
        _inproj_kernel,
        grid=grid,
        in_specs=[
            pl.BlockSpec((tm, D_MODEL), lambda i: (i, 0)),
            pl.BlockSpec((1, D_MODEL), full),
            pl.BlockSpec((D_MODEL, IN_DIM), full, pipeline_mode=pl.Buffered(1)),
            row_spec, row_spec, col_spec, col_spec,
            pl.BlockSpec((1, HEAD_DIM), full),
            pl.BlockSpec((1, HEAD_DIM), full),
        ],
        out_specs=[
            pl.BlockSpec((tm, ATTN_DIM), lambda i: (i, 0)),
            pl.BlockSpec((tm, KV_DIM), lambda i: (i, 0)),
            pl.BlockSpec((KV_HEADS, 1, HEAD_DIM, tm), lambda i: (0, i, 0, 0)),
            pl.BlockSpec((tm, HP_DIM), lambda i: (i, 0)),
        ],
        out_shape=[
            jax.ShapeDtypeStruct((T, ATTN_DIM), BF16),
            jax.ShapeDtypeStruct((T, KV_DIM), BF16),
            jax.ShapeDtypeStruct((KV_HEADS, T // tm, HEAD_DIM, tm), BF16),
            jax.ShapeDtypeStruct((T, HP_DIM), BF16),
        ],
        compiler_params=_cparams(1),
        name="inproj",
    )(x, norm_w, w_in, *tables, q_w, k_w)


def _attn_kernel(q_ref, k_ref, vt_ref, km_ref, vtm_ref, o_ref, qt_sc, s0_sc, s1_sc, acc_sc,
                 *, tq, n_chunks):
    heads = range(GROUP)
    cols = [slice(h * tq, (h + 1) * tq) for h in heads]
    qt = q_ref[...].astype(F32).T
    for h in heads:
        qt_sc[:, cols[h]] = qt[h * 128:(h + 1) * 128, :].astype(BF16)

    subs = ATT_CHUNK // VT_CHUNK

    def scores(j, s_sc):
        start = pl.multiple_of(j * ATT_CHUNK, ATT_CHUNK)
        kc = k_ref[pl.ds(start, ATT_CHUNK), :]
        cm = []
        for h in heads:
            s = jnp.dot(kc, qt_sc[:, cols[h]], preferred_element_type=F32)
            s_sc[:, cols[h]] = s
            cm.append(jnp.max(s, axis=0, keepdims=True))
        return tuple(cm)

    def absorb(j, s_sc, m, l, cm):
        vts = [vt_ref[j * subs + r] for r in range(subs)]
        m_out, l_out = [], []
        for h in heads:
            m_new = jnp.maximum(m[h], cm[h])
            alpha = jnp.exp2(m[h] - m_new)
            p = jnp.exp2(s_sc[:, cols[h]] - m_new)
            l_out.append(alpha * l[h] + jnp.sum(p, axis=0, keepdims=True))
            pb = p.astype(BF16)
            pv = sum(jnp.dot(vts[r], pb[r * VT_CHUNK:(r + 1) * VT_CHUNK],
                             preferred_element_type=F32) for r in range(subs))
            acc_sc[:, cols[h]] = alpha * acc_sc[:, cols[h]] + pv
            m_out.append(m_new)
        return tuple(m_out), tuple(l_out)

    m0, l0 = [], []
    for h in heads:
        s = jnp.dot(km_ref[...], qt_sc[:, cols[h]], preferred_element_type=F32)
        row = lax.broadcasted_iota(jnp.int32, s.shape, 0)
        s = jnp.where(row < N_META, s, NEG_BIG)
        mh = jnp.max(s, axis=0, keepdims=True)
        p = jnp.exp2(s - mh)
        m0.append(mh)
        l0.append(jnp.sum(p, axis=0, keepdims=True))
        acc_sc[:, cols[h]] = jnp.dot(vtm_ref[...], p.astype(BF16), preferred_element_type=F32)
    cm0 = scores(0, s0_sc)

    def pair(i, carry):
        m, l, cm = carry
        cm1 = scores(2 * i + 1, s1_sc)
        m, l = absorb(2 * i, s0_sc, m, l, cm)
        cm2 = scores(2 * i + 2, s0_sc)
        m, l = absorb(2 * i + 1, s1_sc, m, l, cm1)
        return m, l, cm2

    m, l, cm = lax.fori_loop(0, n_chunks // 2 - 1, pair, (tuple(m0), tuple(l0), cm0))
    cm1 = scores(n_chunks - 1, s1_sc)
    m, l = absorb(n_chunks - 2, s0_sc, m, l, cm)
    m, l = absorb(n_chunks - 1, s1_sc, m, l, cm1)

    for h in heads:
        o_t = acc_sc[:, cols[h]] * (1.0 / l[h])
        o_ref[:, h * 128:(h + 1) * 128] = o_t.T.astype(BF16)


def _attention(q, k, vt, km, vtm, B, n, tq):
    Bq, nq, _ = q.shape
    n_chunks = n // ATT_CHUNK
    n_vt = n // VT_CHUNK
    assert n_chunks >= 2 and n_chunks % 2 == 0
    k3 = k.reshape(B, n, KV_DIM)
    qb = (lambda b: b) if Bq > 1 else (lambda b: 0)
    return pl.pallas_call(
        functools.partial(_attn_kernel, tq=tq, n_chunks=n_chunks),
        grid=(B, KV_HEADS, nq // tq),
        in_specs=[
            pl.BlockSpec((None, tq, GROUP * HEAD_DIM), lambda b, g, i: (qb(b), i, g)),
            pl.BlockSpec((None, n, HEAD_DIM), lambda b, g, i: (b, 0, g)),
            pl.BlockSpec((None, n_vt, HEAD_DIM, VT_CHUNK), lambda b, g, i: (g, b, 0, 0)),
            pl.BlockSpec((META_PAD, HEAD_DIM), lambda b, g, i: (0, g)),
            pl.BlockSpec((None, None, HEAD_DIM, META_PAD), lambda b, g, i: (g, 0, 0, 0)),
        ],
        out_specs=pl.BlockSpec((None, tq, GROUP * HEAD_DIM), lambda b, g, i: (b, i, g)),
        out_shape=jax.ShapeDtypeStruct((B, nq, ATTN_DIM), BF16),
        scratch_shapes=[
            pltpu.VMEM((HEAD_DIM, GROUP * tq), BF16),
            pltpu.VMEM((ATT_CHUNK, GROUP * tq), F32),
            pltpu.VMEM((ATT_CHUNK, GROUP * tq), F32),
            pltpu.VMEM((HEAD_DIM, GROUP * tq), F32),
        ],
        compiler_params=_cparams(3),
        name="attention",
    )(q, k3, vt, km, vtm)


HG_CHUNK = 64
HG_GROUP = 128
HG_BLOCK = 512
HG_HPS = 4
HP_Q, HP_FF, HP_FB, HP_I, HP_G = 0, 8, 16, 24, 32


def _layer0_lower_bound(lb_ref):
    lb = lb_ref[...]
    e = jnp.exp(lb - jnp.max(lb, axis=0, keepdims=True))
    return e[0:1, :] / jnp.sum(e, axis=0, keepdims=True)


def _gla_block(q, z, v, lb, s, reverse, n_valid=None):
    tb = q.shape[0]
    C, G = HG_CHUNK, HG_GROUP
    nc, ng = tb // C, tb // G
    tn = (((0,), (0,)), ((), ()))

    f = lb + (1.0 - lb) * jax.nn.sigmoid(z)
    kk = 1.0 - f
    g = jnp.log(f)
    if n_valid is not None:
        valid = lax.broadcasted_iota(jnp.int32, g.shape, 0) < n_valid
        g = jnp.where(valid, g, 0.0)
        kk = jnp.where(valid, kk, 0.0)

    ri = lax.broadcasted_iota(jnp.int32, (G, G), 0)
    ci = lax.broadcasted_iota(jnp.int32, (G, G), 1)
    causal = (ri <= ci) if reverse else (ri >= ci)
    keep = jnp.logical_and(ri // C == ci // C, causal)
    tri = jnp.where(keep, 1.0, 0.0).astype(BF16)

    g_hi = g.astype(BF16)
    g_lo = (g - g_hi.astype(F32)).astype(BF16)
    ghl = jnp.concatenate([g_hi, g_lo], axis=1)
    b_parts = []
    for gi in range(ng):
        part = jnp.dot(tri, ghl[gi * G:(gi + 1) * G], preferred_element_type=F32)
        b_parts.append(part[:, :HEAD_DIM] + part[:, HEAD_DIM:])
    b = jnp.concatenate(b_parts, axis=0) if ng > 1 else b_parts[0]

    def chunk_rows(idx):
        return [b[c * C + idx:c * C + idx + 1, :] for c in range(nc)]

    def spread(rows):
        return jnp.concatenate([jnp.broadcast_to(x, (C, HEAD_DIM)) for x in rows], axis=0)

    r_rows = chunk_rows(C // 2)
    end_rows = chunk_rows(0 if reverse else C - 1)
    r_all = spread(r_rows)
    q_t = q * (HEAD_DIM ** -0.5) * jnp.exp(b - r_all)
    k_t = kk * jnp.exp(r_all - b)
    q_e = (q_t * spread([jnp.exp(r) for r in r_rows])).astype(BF16)
    k_e = (k_t * spread([jnp.exp(e - r) for e, r in zip(end_rows, r_rows)])).astype(BF16)
    q_tb = q_t.astype(BF16)
    k_tb = k_t.astype(BF16)
    vb = v.astype(BF16)

    o_intra = []
    for gi in range(ng):
        sl = slice(gi * G, (gi + 1) * G)
        a = jnp.dot(q_tb[sl], k_t[sl].T.astype(BF16), preferred_element_type=F32)
        a = jnp.where(keep, a, 0.0).astype(BF16)
        o_intra.append(jnp.dot(a, vb[sl], preferred_element_type=F32))

    ends = jnp.concatenate(end_rows + [jnp.zeros((HEAD_DIM - nc, HEAD_DIM), F32)], axis=0)
    decay_t = jnp.exp(ends).T

    o_inter = [None] * nc
    for c in (reversed(range(nc)) if reverse else range(nc)):
        sl = slice(c * C, (c + 1) * C)
        o_inter[c] = jnp.dot(q_e[sl], s.astype(BF16), preferred_element_type=F32)
        kv = lax.dot_general(k_e[sl], vb[sl], tn, preferred_element_type=F32)
        s = s * decay_t[:, c:c + 1] + kv
    o = jnp.concatenate(o_intra, axis=0) + jnp.concatenate(o_inter, axis=0)
    return o, s


def _head_cols(h):
    return slice(h * HEAD_DIM, (h + 1) * HEAD_DIM)


def _hgrn_fwd_kernel(q_ref, z_ref, v_ref, qm_ref, zm_ref, vm_ref, lb_ref,
                     o_ref, om_ref, s_sc):
    i = pl.program_id(2)
    lb = _layer0_lower_bound(lb_ref)

    @pl.when(i == 0)
    def _():
        for h in range(HG_HPS):
            cs = _head_cols(h)
            o_m, s1 = _gla_block(qm_ref[:, cs].astype(F32), zm_ref[:, cs].astype(F32),
                                 vm_ref[:, cs].astype(F32), lb[:, cs],
                                 jnp.zeros((HEAD_DIM, HEAD_DIM), F32), False, n_valid=N_META)
            om_ref[:, cs] = o_m
            s_sc[h] = s1

    for h in range(HG_HPS):
        cs = _head_cols(h)
        o, s_t = _gla_block(q_ref[:, cs].astype(F32), z_ref[:, cs].astype(F32),
                            v_ref[:, cs].astype(F32), lb[:, cs], s_sc[h], False)
        o_ref[:, cs] = o
        s_sc[h] = s_t


def _hgrn_bwd_kernel(q_ref, z_ref, v_ref, gate_ref, of_ref,
                     qm_ref, zm_ref, vm_ref, gatem_ref, ofm_ref, lb_ref, nw_ref,
                     out_ref, outm_ref, s_sc):
    i = pl.program_id(2)
    last = pl.num_programs(2) - 1
    lb = _layer0_lower_bound(lb_ref)
    nw = nw_ref[...]

    def finish(o_f, o_b, gate, w):
        return (_rms(o_f + o_b, w) * jax.nn.silu(gate.astype(F32))).astype(BF16)

    @pl.when(i == 0)
    def _():
        s_sc[...] = jnp.zeros_like(s_sc)
        outm_ref[...] = jnp.zeros_like(outm_ref)

    for h in range(HG_HPS):
        cs = _head_cols(h)
        o, s_t = _gla_block(q_ref[:, cs].astype(F32), z_ref[:, cs].astype(F32),
                            v_ref[:, cs].astype(F32), lb[:, cs], s_sc[h], True)
        out_ref[:, cs] = finish(of_ref[:, cs], o, gate_ref[:, cs], nw[:, cs])
        s_sc[h] = s_t

    @pl.when(i == last)
    def _():
        for h in range(HG_HPS):
            cs = _head_cols(h)
            o_m, _ = _gla_block(qm_ref[:, cs].astype(F32), zm_ref[:, cs].astype(F32),
                                vm_ref[:, cs].astype(F32), lb[:, cs], s_sc[h], True,
                                n_valid=N_META)
            outm_ref[:, cs] = finish(ofm_ref[:, cs], o_m, gatem_ref[:, cs], nw[:, cs])


def _hgrn(hp, hpm, lb_fwd, lb_bwd, hg_norm_w, B, n):
    tb = HG_BLOCK
    nb = n // tb
    W = HG_HPS * HEAD_DIM
    grid = (B, HG_HEADS // HG_HPS, nb)

    def tok(off, rev):
        cb = off // HG_HPS
        if rev:
            return pl.BlockSpec((tb, W), lambda b, h, i: (b * nb + nb - 1 - i, cb + h))
        return pl.BlockSpec((tb, W), lambda b, h, i: (b * nb + i, cb + h))

    def met(off):
        cb = off // HG_HPS
        return pl.BlockSpec((META_PAD, W), lambda b, h, i: (0, cb + h))

    lb_spec = pl.BlockSpec((lb_fwd.shape[0], W), lambda b, h, i: (0, h))
    meta_out = pl.BlockSpec((None, META_PAD, W), lambda b, h, i: (b, 0, h))
    state = pltpu.VMEM((HG_HPS, HEAD_DIM, HEAD_DIM), F32)
    o_f, o_fm = pl.pallas_call(
        _hgrn_fwd_kernel,
        grid=grid,
        in_specs=[tok(HP_Q, False), tok(HP_FF, False), tok(HP_I, False),
                  met(HP_Q), met(HP_FF), met(HP_I), lb_spec],
        out_specs=[pl.BlockSpec((tb, W), lambda b, h, i: (b * nb + i, h)), meta_out],
        out_shape=[
            jax.ShapeDtypeStruct((B * n, HG_DIM), F32),
            jax.ShapeDtypeStruct((B, META_PAD, HG_DIM), F32),
        ],
        scratch_shapes=[state],
        compiler_params=_cparams(3),
        name="hgrn_fwd",
    )(hp, hp, hp, hpm, hpm, hpm, lb_fwd)

    out, out_m = pl.pallas_call(
        _hgrn_bwd_kernel,
        grid=grid,
        in_specs=[tok(HP_Q, True), tok(HP_FB, True), tok(HP_I, True), tok(HP_G, True),
                  pl.BlockSpec((tb, W), lambda b, h, i: (b * nb + nb - 1 - i, h)),
                  met(HP_Q), met(HP_FB), met(HP_I), met(HP_G), meta_out,
                  lb_spec,
                  pl.BlockSpec((1, W), lambda b, h, i: (0, h))],
        out_specs=[pl.BlockSpec((tb, W), lambda b, h, i: (b * nb + nb - 1 - i, h)), meta_out],
        out_shape=[
            jax.ShapeDtypeStruct((B * n, HG_DIM), BF16),
            jax.ShapeDtypeStruct((B, META_PAD, HG_DIM), BF16),
        ],
        scratch_shapes=[state],
        compiler_params=_cparams(3),
        name="hgrn_bwd",
    )(hp, hp, hp, hp, o_f, hpm, hpm, hpm, hpm, o_fm, lb_bwd, hg_norm_w)
    return out, out_m


def _outproj_kernel(x_ref, a_ref, g_ref, wa_ref, wg_ref, o_ref):
    o_ref[...] = (x_ref[...]
                  + jnp.dot(a_ref[...], wa_ref[...], preferred_element_type=F32)
                  + jnp.dot(g_ref[...], wg_ref[...], preferred_element_type=F32))


def _outproj(x, attn, hg, w_out, tm):
    T = x.shape[0]
    return pl.pallas_call(
        _outproj_kernel,
        grid=(T // tm,),
        in_specs=[
            pl.BlockSpec((tm, D_MODEL), lambda i: (i, 0)),
            pl.BlockSpec((tm, ATTN_DIM), lambda i: (i, 0)),
            pl.BlockSpec((tm, HG_DIM), lambda i: (i, 0)),
            pl.BlockSpec((ATTN_DIM, D_MODEL), lambda i: (0, 0), pipeline_mode=pl.Buffered(1)),
            pl.BlockSpec((HG_DIM, D_MODEL), lambda i: (1, 0), pipeline_mode=pl.Buffered(1)),
        ],
        out_specs=pl.BlockSpec((tm, D_MODEL), lambda i: (i, 0)),
        out_shape=jax.ShapeDtypeStruct((T, D_MODEL), F32),
        compiler_params=_cparams(1),
        name="outproj",
    )(x, attn, hg, w_out, w_out)


FFN_TF = 512
HALO = 8
FFN_TAIL = 2 * HALO
FFN_PIECE = 256


def _ffn_kernel(h_ref, hprev_ref, hnext_ref, hmeta_ref, nw_ref, wv_ref, wg_ref,
                cwv_ref, cwg_ref, cbv_ref, cbg_ref, wd_ref, o_ref,
                hn_sc, act0_sc, act1_sc, *, tm, nf):
    i = pl.program_id(1)
    f = pl.program_id(2)
    last_i = pl.num_programs(1) - 1
    rows = tm + FFN_TAIL
    tf = wv_ref.shape[1]

    def conv(u, cw, cb):
        return (pltpu.roll(u, 1, 0)[0:tm] * cw[0:1] + u[0:tm] * cw[1:2]
                + pltpu.roll(u, rows - 1, 0)[0:tm] * cw[2:3] + cb)

    def activate(act_sc):
        hn = hn_sc[...]
        for c in range(0, tf, FFN_PIECE):
            cs = slice(c, c + FFN_PIECE)
            uv = jnp.dot(hn, wv_ref[:, cs], preferred_element_type=F32)
            ug = jnp.dot(hn, wg_ref[:, cs], preferred_element_type=F32)
            val = conv(uv, cwv_ref[:, cs], cbv_ref[:, cs])
            gate = conv(ug, cwg_ref[:, cs], cbg_ref[:, cs])
            act_sc[:, cs] = (jax.nn.silu(gate) * val).astype(BF16)

    def project_down(act_sc):
        o_ref[...] += jnp.dot(act_sc[...], wd_ref[...], preferred_element_type=F32)

    even = f % 2 == 0
    both = jnp.logical_and

    @pl.when(f == 0)
    def _():
        nw = nw_ref[...]
        h = h_ref[...]
        hn_sc[0:tm, :] = _rms(h, nw).astype(BF16)
        nxt = jnp.where(i == last_i, 0.0, _rms(hnext_ref[...], nw))
        prev = _rms(jnp.where(i == 0, hmeta_ref[...], hprev_ref[...]), nw)
        hn_sc[tm:rows, :] = jnp.concatenate([nxt, prev], axis=0).astype(BF16)
        o_ref[...] = h
        activate(act0_sc)

    @pl.when(both(even, both(f > 0, f < nf)))
    def _():
        activate(act0_sc)
        project_down(act1_sc)

    @pl.when(both(jnp.logical_not(even), f < nf))
    def _():
        activate(act1_sc)
        project_down(act0_sc)

    @pl.when(f == nf)
    def _():
        project_down(act0_sc if nf % 2 == 1 else act1_sc)


def _ffn(h1, h1m, norm_w, w_up, conv_w, conv_b, w_down, B, n, tm):
    tf = FFN_TF
    nf = D_FF // tf
    ni = n // tm
    rb = tm // HALO
    n_halo = n // HALO
    up_tile = lambda f: jnp.minimum(f, nf - 1)
    down_tile = lambda f: jnp.maximum(f - 1, 0)
    act_buf = pltpu.VMEM((tm, tf), BF16)
    return pl.pallas_call(
        functools.partial(_ffn_kernel, tm=tm, nf=nf),
        grid=(B, ni, nf + 1),
        in_specs=[
            pl.BlockSpec((None, tm, D_MODEL), lambda b, i, f: (b, i, 0)),
            pl.BlockSpec((None, HALO, D_MODEL), lambda b, i, f: (b, jnp.maximum(i * rb - 1, 0), 0)),
            pl.BlockSpec((None, HALO, D_MODEL),
                         lambda b, i, f: (b, jnp.minimum((i + 1) * rb, n_halo - 1), 0)),
            pl.BlockSpec((None, HALO, D_MODEL), lambda b, i, f: (b, N_META // HALO - 1, 0)),
            pl.BlockSpec((1, D_MODEL), lambda b, i, f: (0, 0)),
            pl.BlockSpec((D_MODEL, tf), lambda b, i, f: (0, up_tile(f))),
            pl.BlockSpec((D_MODEL, tf), lambda b, i, f: (0, nf + up_tile(f))),
            pl.BlockSpec((3, tf), lambda b, i, f: (0, up_tile(f))),
            pl.BlockSpec((3, tf), lambda b, i, f: (0, nf + up_tile(f))),
            pl.BlockSpec((1, tf), lambda b, i, f: (0, up_tile(f))),
            pl.BlockSpec((1, tf), lambda b, i, f: (0, nf + up_tile(f))),
            pl.BlockSpec((tf, D_MODEL), lambda b, i, f: (down_tile(f), 0)),
        ],
        out_specs=pl.BlockSpec((None, tm, D_MODEL), lambda b, i, f: (b, i, 0)),
        out_shape=jax.ShapeDtypeStruct((B, n, D_MODEL), F32),
        scratch_shapes=[pltpu.VMEM((tm + FFN_TAIL, D_MODEL), BF16), act_buf, act_buf],
        compiler_params=_cparams(3),
        name="ffn",
    )(h1, h1, h1, h1m, norm_w, w_up, w_up, conv_w, conv_w, conv_b, conv_b, w_down)


def _rope_tables(n):
    n_freq = HEAD_DIM // 4
    inv_freq = jnp.power(ROPE_THETA, -jnp.arange(n_freq, dtype=F32) / n_freq)
    ang_r = jnp.arange(n // GRID_W, dtype=F32)[:, None] * inv_freq
    ang_c = jnp.arange(GRID_W, dtype=F32)[:, None] * inv_freq
    zr = jnp.zeros((n // GRID_W, HEAD_DIM // 2), F32)
    zc = jnp.zeros((GRID_W, HEAD_DIM // 2), F32)
    cos_row = jnp.concatenate([jnp.cos(ang_r), jnp.cos(ang_r), zr], axis=-1)
    sin_row = jnp.concatenate([-jnp.sin(ang_r), jnp.sin(ang_r), zr], axis=-1)
    cos_col = jnp.concatenate([zc, jnp.cos(ang_c), jnp.cos(ang_c)], axis=-1)
    sin_col = jnp.concatenate([zc, -jnp.sin(ang_c), jnp.sin(ang_c)], axis=-1)
    return cos_row, sin_row, cos_col, sin_col


def _identity_rope_tables(n):
    half = HEAD_DIM // 2
    ones_lo = jnp.concatenate([jnp.ones((1, half), F32), jnp.zeros((1, half), F32)], axis=-1)
    ones_hi = jnp.concatenate([jnp.zeros((1, half), F32), jnp.ones((1, half), F32)], axis=-1)
    return (jnp.tile(ones_lo, (n // GRID_W, 1)), jnp.zeros((n // GRID_W, HEAD_DIM), F32),
            jnp.tile(ones_hi, (GRID_W, 1)), jnp.zeros((GRID_W, HEAD_DIM), F32))


def _trunk(x, meta, mp, wts, tables, tiles):
    (w_in, w_out, norm_mix_w, q_w, k_w, hg_w, lb_fwd, lb_bwd,
     norm_ffn_w, w_up, conv_w, conv_b, w_down) = wts
    q_m, k_m, vt_m, hp_m = mp
    B, n, _ = x.shape
    T = B * n
    xf = x.reshape(T, D_MODEL)

    q, k, vt, hp = _inproj(xf, norm_mix_w, w_in, tables, q_w, k_w, VT_CHUNK, n)
    attn = _attention(q.reshape(B, n, ATTN_DIM), k, vt, k_m, vt_m, B, n, tiles["tq"])
    attn_m = _attention(q_m.reshape(1, META_PAD, ATTN_DIM), k, vt, k_m, vt_m, B, n, META_PAD)
    hg, hg_m = _hgrn(hp, hp_m, lb_fwd, lb_bwd, hg_w, B, n)

    h1 = _outproj(xf, attn.reshape(T, ATTN_DIM), hg, w_out, tiles["tm_out"])
    x_m = jnp.broadcast_to(meta[None], (B, N_META, D_MODEL)).reshape(B * N_META, D_MODEL)
    h1_m = _outproj(x_m, attn_m[:, :N_META].reshape(B * N_META, ATTN_DIM),
                    hg_m[:, :N_META].reshape(B * N_META, HG_DIM), w_out, B * N_META)

    return _ffn(h1.reshape(B, n, D_MODEL), h1_m.reshape(B, N_META, D_MODEL),
                norm_ffn_w, w_up, conv_w, conv_b, w_down, B, n, tiles["tm_ffn"])


def _forward(x_prompt, x_sample, meta_tokens, w_in, w_out, norm_mix_w, q_norm_w, k_norm_w,
             hg_norm_w, lb_fwd, lb_bwd, norm_ffn_w, w_up, conv_w, conv_b, w_down, tiles):
    wts = (w_in[0].astype(BF16), w_out[0].astype(BF16), norm_mix_w[0][None], q_norm_w[0][None],
           k_norm_w[0][None], hg_norm_w[0][None], lb_fwd, lb_bwd, norm_ffn_w[0][None],
           w_up[0].astype(BF16), conv_w[0], conv_b[0][None], w_down[0].astype(BF16))
    tables = _rope_tables(max(x_prompt.shape[1], x_sample.shape[1]))

    meta_pad = jnp.zeros((META_PAD, D_MODEL), F32).at[:N_META].set(meta_tokens)
    mp = _inproj(meta_pad, wts[2], wts[0], _identity_rope_tables(META_PAD), wts[3], wts[4],
                 META_PAD, META_PAD)

    y_p = _trunk(x_prompt, meta_tokens, mp, wts, tables, tiles)
    y_s = _trunk(x_sample, meta_tokens, mp, wts, tables, tiles)
    return (y_p, y_s)


TILES = {"tq": 512, "tm_out": 512, "tm_ffn": 512}


def kernel(x_prompt, x_sample, meta_tokens, w_in, w_out, norm_mix_w, q_norm_w, k_norm_w,
           hg_norm_w, lb_fwd, lb_bwd, norm_ffn_w, w_up, conv_w, conv_b, w_down):
    return _forward(x_prompt, x_sample, meta_tokens, w_in, w_out, norm_mix_w, q_norm_w,
                    k_norm_w, hg_norm_w, lb_fwd, lb_bwd, norm_ffn_w, w_up, conv_w, conv_b,
                    w_down, TILES)
```

```python
import functools
import math

import jax
import jax.numpy as jnp
from jax import lax
from jax.experimental import pallas as pl
from jax.experimental.pallas import tpu as pltpu

D_MODEL = 2048
N_META = 16
GRID_W = 64
HEAD_DIM = 128
ATTN_HEADS = 8
KV_HEADS = 2
GROUP = ATTN_HEADS // KV_HEADS
ATTN_DIM = ATTN_HEADS * HEAD_DIM
KV_DIM = KV_HEADS * HEAD_DIM
HG_HEADS = 8
HG_DIM = HG_HEADS * HEAD_DIM
HP_DIM = 5 * HG_DIM
IN_DIM = ATTN_DIM + 2 * KV_DIM + HP_DIM
D_FF = 5632
ROPE_THETA = 10000.0
EPS = 1e-6

META_PAD = 128
VT_CHUNK = 512
ATT_CHUNK = 512
NEG_BIG = -1e30
VMEM_LIMIT = 56 * 1024 * 1024

F32 = jnp.float32
BF16 = jnp.bfloat16


def _cparams(n_axes):
    return pltpu.CompilerParams(
        dimension_semantics=("arbitrary",) * n_axes, vmem_limit_bytes=VMEM_LIMIT)


def _rms(x, w):
    return x * lax.rsqrt(jnp.mean(x * x, axis=-1, keepdims=True) + EPS) * w


IN_CHUNK = 512
Q_SCALE = HEAD_DIM ** -0.5 * math.log2(math.e)


def _norm_rope(t, w, cos, sin):
    y = _rms(t, w)
    lane = lax.broadcasted_iota(jnp.int32, y.shape, 1)
    partner = jnp.where((lane % 64) < 32,
                        pltpu.roll(y, HEAD_DIM - 32, 1), pltpu.roll(y, 32, 1))
    return y * cos + partner * sin


def _token_table(row_ref, col_ref):
    col = col_ref[...]
    return jnp.concatenate(
        [jnp.broadcast_to(row_ref[g:g + 1, :], col.shape) + col for g in range(row_ref.shape[0])],
        axis=0)


def _inproj_kernel(x_ref, nw_ref, w_ref, cosr_ref, sinr_ref, cosc_ref, sinc_ref, qw_ref, kw_ref,
                   q_ref, k_ref, vt_ref, hp_ref):
    hn = _rms(x_ref[...], nw_ref[...]).astype(BF16)
    cos = _token_table(cosr_ref, cosc_ref)
    sin = _token_table(sinr_ref, sinc_ref)
    n_chunks = IN_DIM // IN_CHUNK
    for c in range(n_chunks):
        acc = jnp.dot(hn, w_ref[:, c * IN_CHUNK:(c + 1) * IN_CHUNK],
                      preferred_element_type=F32)
        if c < 2:
            for h in range(4):
                t = _norm_rope(acc[:, h * 128:(h + 1) * 128], qw_ref[...], cos, sin)
                col = (c * 4 + h) * 128
                q_ref[:, col:col + 128] = (t * Q_SCALE).astype(BF16)
        elif c == 2:
            for h in range(KV_HEADS):
                t = _norm_rope(acc[:, h * 128:(h + 1) * 128], kw_ref[...], cos, sin)
                k_ref[:, h * 128:(h + 1) * 128] = t.astype(BF16)
            for h in range(KV_HEADS):
                v = acc[:, KV_DIM + h * 128:KV_DIM + (h + 1) * 128]
                vt_ref[h, 0] = v.T.astype(BF16)
        else:
            col = (c - 3) * IN_CHUNK
            hp_ref[:, col:col + IN_CHUNK] = acc.astype(BF16)


def _inproj(x, norm_w, w_in, tables, q_w, k_w, tm, seq_len):
    T = x.shape[0]
    n_tab = seq_len // tm
    tr = tm // GRID_W
    grid = (T // tm,)
    full = lambda i: (0, 0)
    row_spec = pl.BlockSpec((tr, HEAD_DIM), lambda i: (i % n_tab, 0))
    col_spec = pl.BlockSpec((GRID_W, HEAD_DIM), full)
    return pl.pallas_call(
        _inproj_kernel,
        grid=grid,
        in_specs=[
            pl.BlockSpec((tm, D_MODEL), lambda i: (i, 0)),
            pl.BlockSpec((1, D_MODEL), full),
            pl.BlockSpec((D_MODEL, IN_DIM), full, pipeline_mode=pl.Buffered(1)),
            row_spec, row_spec, col_spec, col_spec,
            pl.BlockSpec((1, HEAD_DIM), full),
            pl.BlockSpec((1, HEAD_DIM), full),
        ],
        out_specs=[
            pl.BlockSpec((tm, ATTN_DIM), lambda i: (i, 0)),
            pl.BlockSpec((tm, KV_DIM), lambda i: (i, 0)),
            pl.BlockSpec((KV_HEADS, 1, HEAD_DIM, tm), lambda i: (0, i, 0, 0)),
            pl.BlockSpec((tm, HP_DIM), lambda i: (i, 0)),
        ],
        out_shape=[
            jax.ShapeDtypeStruct((T, ATTN_DIM), BF16),
            jax.ShapeDtypeStruct((T, KV_DIM), BF16),
            jax.ShapeDtypeStruct((KV_HEADS, T // tm, HEAD_DIM, tm), BF16),
            jax.ShapeDtypeStruct((T, HP_DIM), BF16),
        ],
        compiler_params=_cparams(1),
        name="inproj",
    )(x, norm_w, w_in, *tables, q_w, k_w)


def _attn_kernel(q_ref, k_ref, vt_ref, km_ref, vtm_ref, o_ref, qt_sc, s0_sc, s1_sc, acc_sc,
                 *, tq, n_chunks):
    heads = range(GROUP)
    cols = [slice(h * tq, (h + 1) * tq) for h in heads]
    qt = q_ref[...].astype(F32).T
    for h in heads:
        qt_sc[:, cols[h]] = qt[h * 128:(h + 1) * 128, :].astype(BF16)

    subs = ATT_CHUNK // VT_CHUNK

    def scores(j, s_sc):
        start = pl.multiple_of(j * ATT_CHUNK, ATT_CHUNK)
        kc = k_ref[pl.ds(start, ATT_CHUNK), :]
        cm = []
        for h in heads:
            s = jnp.dot(kc, qt_sc[:, cols[h]], preferred_element_type=F32)
            s_sc[:, cols[h]] = s
            cm.append(jnp.max(s, axis=0, keepdims=True))
        return tuple(cm)

    def absorb(j, s_sc, m, l, cm):
        vts = [vt_ref[j * subs + r] for r in range(subs)]
        m_out, l_out = [], []
        for h in heads:
            m_new = jnp.maximum(m[h], cm[h])
            alpha = jnp.exp2(m[h] - m_new)
            p = jnp.exp2(s_sc[:, cols[h]] - m_new)
            l_out.append(alpha * l[h] + jnp.sum(p, axis=0, keepdims=True))
            pb = p.astype(BF16)
            pv = sum(jnp.dot(vts[r], pb[r * VT_CHUNK:(r + 1) * VT_CHUNK],
                             preferred_element_type=F32) for r in range(subs))
            acc_sc[:, cols[h]] = alpha * acc_sc[:, cols[h]] + pv
            m_out.append(m_new)
        return tuple(m_out), tuple(l_out)

    m0, l0 = [], []
    for h in heads:
        s = jnp.dot(km_ref[...], qt_sc[:, cols[h]], preferred_element_type=F32)
        row = lax.broadcasted_iota(jnp.int32, s.shape, 0)
        s = jnp.where(row < N_META, s, NEG_BIG)
        mh = jnp.max(s, axis=0, keepdims=True)
        p = jnp.exp2(s - mh)
        m0.append(mh)
        l0.append(jnp.sum(p, axis=0, keepdims=True))
        acc_sc[:, cols[h]] = jnp.dot(vtm_ref[...], p.astype(BF16), preferred_element_type=F32)
    cm0 = scores(0, s0_sc)

    def pair(i, carry):
        m, l, cm = carry
        cm1 = scores(2 * i + 1, s1_sc)
        m, l = absorb(2 * i, s0_sc, m, l, cm)
        cm2 = scores(2 * i + 2, s0_sc)
        m, l = absorb(2 * i + 1, s1_sc, m, l, cm1)
        return m, l, cm2

    m, l, cm = lax.fori_loop(0, n_chunks // 2 - 1, pair, (tuple(m0), tuple(l0), cm0))
    cm1 = scores(n_chunks - 1, s1_sc)
    m, l = absorb(n_chunks - 2, s0_sc, m, l, cm)
    m, l = absorb(n_chunks - 1, s1_sc, m, l, cm1)

    for h in heads:
        o_t = acc_sc[:, cols[h]] * (1.0 / l[h])
        o_ref[:, h * 128:(h + 1) * 128] = o_t.T.astype(BF16)


def _attention(q, k, vt, km, vtm, B, n, tq):
    Bq, nq, _ = q.shape
    n_chunks = n // ATT_CHUNK
    n_vt = n // VT_CHUNK
    assert n_chunks >= 2 and n_chunks % 2 == 0
    k3 = k.reshape(B, n, KV_DIM)
    qb = (lambda b: b) if Bq > 1 else (lambda b: 0)
    return pl.pallas_call(
        functools.partial(_attn_kernel, tq=tq, n_chunks=n_chunks),
        grid=(B, KV_HEADS, nq // tq),
        in_specs=[
            pl.BlockSpec((None, tq, GROUP * HEAD_DIM), lambda b, g, i: (qb(b), i, g)),
            pl.BlockSpec((None, n, HEAD_DIM), lambda b, g, i: (b, 0, g)),
            pl.BlockSpec((None, n_vt, HEAD_DIM, VT_CHUNK), lambda b, g, i: (g, b, 0, 0)),
            pl.BlockSpec((META_PAD, HEAD_DIM), lambda b, g, i: (0, g)),
            pl.BlockSpec((None, None, HEAD_DIM, META_PAD), lambda b, g, i: (g, 0, 0, 0)),
        ],
        out_specs=pl.BlockSpec((None, tq, GROUP * HEAD_DIM), lambda b, g, i: (b, i, g)),
        out_shape=jax.ShapeDtypeStruct((B, nq, ATTN_DIM), BF16),
        scratch_shapes=[
            pltpu.VMEM((HEAD_DIM, GROUP * tq), BF16),
            pltpu.VMEM((ATT_CHUNK, GROUP * tq), F32),
            pltpu.VMEM((ATT_CHUNK, GROUP * tq), F32),
            pltpu.VMEM((HEAD_DIM, GROUP * tq), F32),
        ],
        compiler_params=_cparams(3),
        name="attention",
    )(q, k3, vt, km, vtm)


HG_CHUNK = 64
HG_GROUP = 128
HG_BLOCK = 512
HG_HPS = 4
HP_Q, HP_FF, HP_FB, HP_I, HP_G = 0, 8, 16, 24, 32


def _layer0_lower_bound(lb_ref):
    lb = lb_ref[...]
    e = jnp.exp(lb - jnp.max(lb, axis=0, keepdims=True))
    return e[0:1, :] / jnp.sum(e, axis=0, keepdims=True)


def _gla_block(q, z, v, lb, s, reverse, n_valid=None):
    tb = q.shape[0]
    C, G = HG_CHUNK, HG_GROUP
    nc, ng = tb // C, tb // G
    tn = (((0,), (0,)), ((), ()))

    f = lb + (1.0 - lb) * jax.nn.sigmoid(z)
    kk = 1.0 - f
    g = jnp.log(f)
    if n_valid is not None:
        valid = lax.broadcasted_iota(jnp.int32, g.shape, 0) < n_valid
        g = jnp.where(valid, g, 0.0)
        kk = jnp.where(valid, kk, 0.0)

    ri = lax.broadcasted_iota(jnp.int32, (G, G), 0)
    ci = lax.broadcasted_iota(jnp.int32, (G, G), 1)
    causal = (ri <= ci) if reverse else (ri >= ci)
    keep = jnp.logical_and(ri // C == ci // C, causal)
    tri = jnp.where(keep, 1.0, 0.0).astype(BF16)

    g_hi = g.astype(BF16)
    g_lo = (g - g_hi.astype(F32)).astype(BF16)
    ghl = jnp.concatenate([g_hi, g_lo], axis=1)
    b_parts = []
    for gi in range(ng):
        part = jnp.dot(tri, ghl[gi * G:(gi + 1) * G], preferred_element_type=F32)
        b_parts.append(part[:, :HEAD_DIM] + part[:, HEAD_DIM:])
    b = jnp.concatenate(b_parts, axis=0) if ng > 1 else b_parts[0]

    def chunk_rows(idx):
        return [b[c * C + idx:c * C + idx + 1, :] for c in range(nc)]

    def spread(rows):
        return jnp.concatenate([jnp.broadcast_to(x, (C, HEAD_DIM)) for x in rows], axis=0)

    r_rows = chunk_rows(C // 2)
    end_rows = chunk_rows(0 if reverse else C - 1)
    r_all = spread(r_rows)
    q_t = q * (HEAD_DIM ** -0.5) * jnp.exp(b - r_all)
    k_t = kk * jnp.exp(r_all - b)
    q_e = (q_t * spread([jnp.exp(r) for r in r_rows])).astype(BF16)
    k_e = (k_t * spread([jnp.exp(e - r) for e, r in zip(end_rows, r_rows)])).astype(BF16)
    q_tb = q_t.astype(BF16)
    k_tb = k_t.astype(BF16)
    vb = v.astype(BF16)

    o_intra = []
    for gi in range(ng):
        sl = slice(gi * G, (gi + 1) * G)
        a = jnp.dot(q_tb[sl], k_t[sl].T.astype(BF16), preferred_element_type=F32)
        a = jnp.where(keep, a, 0.0).astype(BF16)
        o_intra.append(jnp.dot(a, vb[sl], preferred_element_type=F32))

    ends = jnp.concatenate(end_rows + [jnp.zeros((HEAD_DIM - nc, HEAD_DIM), F32)], axis=0)
    decay_t = jnp.exp(ends).T

    o_inter = [None] * nc
    for c in (reversed(range(nc)) if reverse else range(nc)):
        sl = slice(c * C, (c + 1) * C)
        o_inter[c] = jnp.dot(q_e[sl], s.astype(BF16), preferred_element_type=F32)
        kv = lax.dot_general(k_e[sl], vb[sl], tn, preferred_element_type=F32)
        s = s * decay_t[:, c:c + 1] + kv
    o = jnp.concatenate(o_intra, axis=0) + jnp.concatenate(o_inter, axis=0)
    return o, s


def _head_cols(h):
    return slice(h * HEAD_DIM, (h + 1) * HEAD_DIM)


def _hgrn_fwd_kernel(q_ref, z_ref, v_ref, qm_ref, zm_ref, vm_ref, lb_ref,
                     o_ref, om_ref, s_sc):
    i = pl.program_id(2)
    lb = _layer0_lower_bound(lb_ref)

    @pl.when(i == 0)
    def _():
        for h in range(HG_HPS):
            cs = _head_cols(h)
            o_m, s1 = _gla_block(qm_ref[:, cs].astype(F32), zm_ref[:, cs].astype(F32),
                                 vm_ref[:, cs].astype(F32), lb[:, cs],
                                 jnp.zeros((HEAD_DIM, HEAD_DIM), F32), False, n_valid=N_META)
            om_ref[:, cs] = o_m
            s_sc[h] = s1

    for h in range(HG_HPS):
        cs = _head_cols(h)
        o, s_t = _gla_block(q_ref[:, cs].astype(F32), z_ref[:, cs].astype(F32),
                            v_ref[:, cs].astype(F32), lb[:, cs], s_sc[h], False)
        o_ref[:, cs] = o
        s_sc[h] = s_t


def _hgrn_bwd_kernel(q_ref, z_ref, v_ref, gate_ref, of_ref,
                     qm_ref, zm_ref, vm_ref, gatem_ref, ofm_ref, lb_ref, nw_ref,
                     out_ref, outm_ref, s_sc):
    i = pl.program_id(2)
    last = pl.num_programs(2) - 1
    lb = _layer0_lower_bound(lb_ref)
    nw = nw_ref[...]

    def finish(o_f, o_b, gate, w):
        return (_rms(o_f + o_b, w) * jax.nn.silu(gate.astype(F32))).astype(BF16)

    @pl.when(i == 0)
    def _():
        s_sc[...] = jnp.zeros_like(s_sc)
        outm_ref[...] = jnp.zeros_like(outm_ref)

    for h in range(HG_HPS):
        cs = _head_cols(h)
        o, s_t = _gla_block(q_ref[:, cs].astype(F32), z_ref[:, cs].astype(F32),
                            v_ref[:, cs].astype(F32), lb[:, cs], s_sc[h], True)
        out_ref[:, cs] = finish(of_ref[:, cs], o, gate_ref[:, cs], nw[:, cs])
        s_sc[h] = s_t

    @pl.when(i == last)
    def _():
        for h in range(HG_HPS):
            cs = _head_cols(h)
            o_m, _ = _gla_block(qm_ref[:, cs].astype(F32), zm_ref[:, cs].astype(F32),
                                vm_ref[:, cs].astype(F32), lb[:, cs], s_sc[h], True,
                                n_valid=N_META)
            outm_ref[:, cs] = finish(ofm_ref[:, cs], o_m, gatem_ref[:, cs], nw[:, cs])


def _hgrn(hp, hpm, lb_fwd, lb_bwd, hg_norm_w, B, n):
    tb = HG_BLOCK
    nb = n // tb
    W = HG_HPS * HEAD_DIM
    grid = (B, HG_HEADS // HG_HPS, nb)

    def tok(off, rev):
        cb = off // HG_HPS
        if rev:
            return pl.BlockSpec((tb, W), lambda b, h, i: (b * nb + nb - 1 - i, cb + h))
        return pl.BlockSpec((tb, W), lambda b, h, i: (b * nb + i, cb + h))

    def met(off):
        cb = off // HG_HPS
        return pl.BlockSpec((META_PAD, W), lambda b, h, i: (0, cb + h))

    lb_spec = pl.BlockSpec((lb_fwd.shape[0], W), lambda b, h, i: (0, h))
    meta_out = pl.BlockSpec((None, META_PAD, W), lambda b, h, i: (b, 0, h))
    state = pltpu.VMEM((HG_HPS, HEAD_DIM, HEAD_DIM), F32)
    o_f, o_fm = pl.pallas_call(
        _hgrn_fwd_kernel,
        grid=grid,
        in_specs=[tok(HP_Q, False), tok(HP_FF, False), tok(HP_I, False),
                  met(HP_Q), met(HP_FF), met(HP_I), lb_spec],
        out_specs=[pl.BlockSpec((tb, W), lambda b, h, i: (b * nb + i, h)), meta_out],
        out_shape=[
            jax.ShapeDtypeStruct((B * n, HG_DIM), F32),
            jax.ShapeDtypeStruct((B, META_PAD, HG_DIM), F32),
        ],
        scratch_shapes=[state],
        compiler_params=_cparams(3),
        name="hgrn_fwd",
    )(hp, hp, hp, hpm, hpm, hpm, lb_fwd)

    out, out_m = pl.pallas_call(
        _hgrn_bwd_kernel,
        grid=grid,
        in_specs=[tok(HP_Q, True), tok(HP_FB, True), tok(HP_I, True), tok(HP_G, True),
                  pl.BlockSpec((tb, W), lambda b, h, i: (b * nb + nb - 1 - i, h)),
                  met(HP_Q), met(HP_FB), met(HP_I), met(HP_G), meta_out,
                  lb_spec,
                  pl.BlockSpec((1, W), lambda b, h, i: (0, h))],
        out_specs=[pl.BlockSpec((tb, W), lambda b, h, i: (b * nb + nb - 1 - i, h)), meta_out],
        out_shape=[
            jax.ShapeDtypeStruct((B * n, HG_DIM), BF16),
            jax.ShapeDtypeStruct((B, META_PAD, HG_DIM), BF16),
        ],
        scratch_shapes=[state],
        compiler_params=_cparams(3),
        name="hgrn_bwd",
    )(hp, hp, hp, hp, o_f, hpm, hpm, hpm, hpm, o_fm, lb_bwd, hg_norm_w)
    return out, out_m


def _outproj_kernel(x_ref, a_ref, g_ref, wa_ref, wg_ref, o_ref):
    o_ref[...] = (x_ref[...]
                  + jnp.dot(a_ref[...], wa_ref[...], preferred_element_type=F32)
                  + jnp.dot(g_ref[...], wg_ref[...], preferred_element_type=F32))


def _outproj(x, attn, hg, w_out, tm):
    T = x.shape[0]
    return pl.pallas_call(
        _outproj_kernel,
        grid=(T // tm,),
        in_specs=[
            pl.BlockSpec((tm, D_MODEL), lambda i: (i, 0)),
            pl.BlockSpec((tm, ATTN_DIM), lambda i: (i, 0)),
            pl.BlockSpec((tm, HG_DIM), lambda i: (i, 0)),
            pl.BlockSpec((ATTN_DIM, D_MODEL), lambda i: (0, 0), pipeline_mode=pl.Buffered(1)),
            pl.BlockSpec((HG_DIM, D_MODEL), lambda i: (1, 0), pipeline_mode=pl.Buffered(1)),
        ],
        out_specs=pl.BlockSpec((tm, D_MODEL), lambda i: (i, 0)),
        out_shape=jax.ShapeDtypeStruct((T, D_MODEL), F32),
        compiler_params=_cparams(1),
        name="outproj",
    )(x, attn, hg, w_out, w_out)


FFN_TF = 512
HALO = 16


def _ffn_kernel(h_ref, hprev_ref, hnext_ref, hmeta_ref, nw_ref, wv_ref, wg_ref,
                cwv_ref, cwg_ref, cbv_ref, cbg_ref, wd_ref, o_ref, hn_sc, *, tm):
    i = pl.program_id(1)
    f = pl.program_id(2)
    last_i = pl.num_programs(1) - 1

    @pl.when(f == 0)
    def _():
        nw = nw_ref[...]
        h = h_ref[...]
        prev = jnp.where(i == 0, hmeta_ref[...], hprev_ref[...])
        hn_sc[0:HALO, :] = _rms(prev, nw).astype(BF16)
        hn_sc[HALO:HALO + tm, :] = _rms(h, nw).astype(BF16)
        nxt = jnp.where(i == last_i, 0.0, _rms(hnext_ref[...], nw))
        hn_sc[HALO + tm:2 * HALO + tm, :] = nxt.astype(BF16)
        o_ref[...] = h

    hn = hn_sc[...]

    def conv(w_ref, cw_ref, cb_ref):
        u = jnp.dot(hn, w_ref[...], preferred_element_type=F32)
        cw = cw_ref[...]
        return (u[HALO - 1:HALO - 1 + tm] * cw[0:1] + u[HALO:HALO + tm] * cw[1:2]
                + u[HALO + 1:HALO + 1 + tm] * cw[2:3] + cb_ref[...])

    val = conv(wv_ref, cwv_ref, cbv_ref)
    gate = conv(wg_ref, cwg_ref, cbg_ref)
    act = (jax.nn.silu(gate) * val).astype(BF16)
    o_ref[...] += jnp.dot(act, wd_ref[...], preferred_element_type=F32)


def _ffn(h1, h1m, norm_w, w_up, conv_w, conv_b, w_down, B, n, tm):
    tf = FFN_TF
    nf = D_FF // tf
    ni = n // tm
    rb = tm // HALO
    n_halo = n // HALO
    return pl.pallas_call(
        functools.partial(_ffn_kernel, tm=tm),
        grid=(B, ni, nf),
        in_specs=[
            pl.BlockSpec((None, tm, D_MODEL), lambda b, i, f: (b, i, 0)),
            pl.BlockSpec((None, HALO, D_MODEL), lambda b, i, f: (b, jnp.maximum(i * rb - 1, 0), 0)),
            pl.BlockSpec((None, HALO, D_MODEL),
                         lambda b, i, f: (b, jnp.minimum((i + 1) * rb, n_halo - 1), 0)),
            pl.BlockSpec((None, N_META, D_MODEL), lambda b, i, f: (b, 0, 0)),
            pl.BlockSpec((1, D_MODEL), lambda b, i, f: (0, 0)),
            pl.BlockSpec((D_MODEL, tf), lambda b, i, f: (0, f)),
            pl.BlockSpec((D_MODEL, tf), lambda b, i, f: (0, nf + f)),
            pl.BlockSpec((3, tf), lambda b, i, f: (0, f)),
            pl.BlockSpec((3, tf), lambda b, i, f: (0, nf + f)),
            pl.BlockSpec((1, tf), lambda b, i, f: (0, f)),
            pl.BlockSpec((1, tf), lambda b, i, f: (0, nf + f)),
            pl.BlockSpec((tf, D_MODEL), lambda b, i, f: (f, 0)),
        ],
        out_specs=pl.BlockSpec((None, tm, D_MODEL), lambda b, i, f: (b, i, 0)),
        out_shape=jax.ShapeDtypeStruct((B, n, D_MODEL), F32),
        scratch_shapes=[pltpu.VMEM((tm + 2 * HALO, D_MODEL), BF16)],
        compiler_params=_cparams(3),
        name="ffn",
    )(h1, h1, h1, h1m, norm_w, w_up, w_up, conv_w, conv_w, conv_b, conv_b, w_down)


def _rope_tables(n):
    n_freq = HEAD_DIM // 4
    inv_freq = jnp.power(ROPE_THETA, -jnp.arange(n_freq, dtype=F32) / n_freq)
    ang_r = jnp.arange(n // GRID_W, dtype=F32)[:, None] * inv_freq
    ang_c = jnp.arange(GRID_W, dtype=F32)[:, None] * inv_freq
    zr = jnp.zeros((n // GRID_W, HEAD_DIM // 2), F32)
    zc = jnp.zeros((GRID_W, HEAD_DIM // 2), F32)
    cos_row = jnp.concatenate([jnp.cos(ang_r), jnp.cos(ang_r), zr], axis=-1)
    sin_row = jnp.concatenate([-jnp.sin(ang_r), jnp.sin(ang_r), zr], axis=-1)
    cos_col = jnp.concatenate([zc, jnp.cos(ang_c), jnp.cos(ang_c)], axis=-1)
    sin_col = jnp.concatenate([zc, -jnp.sin(ang_c), jnp.sin(ang_c)], axis=-1)
    return cos_row, sin_row, cos_col, sin_col


def _identity_rope_tables(n):
    half = HEAD_DIM // 2
    ones_lo = jnp.concatenate([jnp.ones((1, half), F32), jnp.zeros((1, half), F32)], axis=-1)
    ones_hi = jnp.concatenate([jnp.zeros((1, half), F32), jnp.ones((1, half), F32)], axis=-1)
    return (jnp.tile(ones_lo, (n // GRID_W, 1)), jnp.zeros((n // GRID_W, HEAD_DIM), F32),
            jnp.tile(ones_hi, (GRID_W, 1)), jnp.zeros((GRID_W, HEAD_DIM), F32))


def _trunk(x, meta, mp, wts, tables, tiles):
    (w_in, w_out, norm_mix_w, q_w, k_w, hg_w, lb_fwd, lb_bwd,
     norm_ffn_w, w_up, conv_w, conv_b, w_down) = wts
    q_m, k_m, vt_m, hp_m = mp
    B, n, _ = x.shape
    T = B * n
    xf = x.reshape(T, D_MODEL)

    q, k, vt, hp = _inproj(xf, norm_mix_w, w_in, tables, q_w, k_w, VT_CHUNK, n)
    attn = _attention(q.reshape(B, n, ATTN_DIM), k, vt, k_m, vt_m, B, n, tiles["tq"])
    attn_m = _attention(q_m.reshape(1, META_PAD, ATTN_DIM), k, vt, k_m, vt_m, B, n, META_PAD)
    hg, hg_m = _hgrn(hp, hp_m, lb_fwd, lb_bwd, hg_w, B, n)

    h1 = _outproj(xf, attn.reshape(T, ATTN_DIM), hg, w_out, tiles["tm_out"])
    x_m = jnp.broadcast_to(meta[None], (B, N_META, D_MODEL)).reshape(B * N_META, D_MODEL)
    h1_m = _outproj(x_m, attn_m[:, :N_META].reshape(B * N_META, ATTN_DIM),
                    hg_m[:, :N_META].reshape(B * N_META, HG_DIM), w_out, B * N_META)

    return _ffn(h1.reshape(B, n, D_MODEL), h1_m.reshape(B, N_META, D_MODEL),
                norm_ffn_w, w_up, conv_w, conv_b, w_down, B, n, tiles["tm_ffn"])


def _forward(x_prompt, x_sample, meta_tokens, w_in, w_out, norm_mix_w, q_norm_w, k_norm_w,
             hg_norm_w, lb_fwd, lb_bwd, norm_ffn_w, w_up, conv_w, conv_b, w_down, tiles):
    wts = (w_in[0].astype(BF16), w_out[0].astype(BF16), norm_mix_w[0][None], q_norm_w[0][None],
           k_norm_w[0][None], hg_norm_w[0][None], lb_fwd, lb_bwd, norm_ffn_w[0][None],
           w_up[0].astype(BF16), conv_w[0], conv_b[0][None], w_down[0].astype(BF16))
    tables = _rope_tables(max(x_prompt.shape[1], x_sample.shape[1]))

    meta_pad = jnp.zeros((META_PAD, D_MODEL), F32).at[:N_META].set(meta_tokens)
    mp = _inproj(meta_pad, wts[2], wts[0], _identity_rope_tables(META_PAD), wts[3], wts[4],
                 META_PAD, META_PAD)

    y_p = _trunk(x_prompt, meta_tokens, mp, wts, tables, tiles)
    y_s = _trunk(x_sample, meta_tokens, mp, wts, tables, tiles)
    return (y_p, y_s)


TILES = {"tq": 512, "tm_out": 512, "tm_ffn": 1024}


def kernel(x_prompt, x_sample, meta_tokens, w_in, w_out, norm_mix_w, q_norm_w, k_norm_w,
           hg_norm_w, lb_fwd, lb_bwd, norm_ffn_w, w_up, conv_w, conv_b, w_down):
    return _forward(x_prompt, x_sample, meta_tokens, w_in, w_out, norm_mix_w, q_norm_w,
                    k_norm_w, hg_norm_w, lb_fwd, lb_bwd, norm_ffn_w, w_up, conv_w, conv_b,
                    w_down, TILES)
```

```python
import functools
import math

import jax
import jax.numpy as jnp
from jax import lax
from jax.experimental import pallas as pl
from jax.experimental.pallas import tpu as pltpu

D_MODEL = 2048
N_META = 16
GRID_W = 64
HEAD_DIM = 128
ATTN_HEADS = 8
KV_HEADS = 2
GROUP = ATTN_HEADS // KV_HEADS
ATTN_DIM = ATTN_HEADS * HEAD_DIM
KV_DIM = KV_HEADS * HEAD_DIM
HG_HEADS = 8
HG_DIM = HG_HEADS * HEAD_DIM
HP_DIM = 5 * HG_DIM
IN_DIM = ATTN_DIM + 2 * KV_DIM + HP_DIM
D_FF = 5632
ROPE_THETA = 10000.0
EPS = 1e-6

META_PAD = 128
VT_CHUNK = 512
ATT_CHUNK = 512
NEG_BIG = -1e30
VMEM_LIMIT = 56 * 1024 * 1024

F32 = jnp.float32
BF16 = jnp.bfloat16


def _cparams(n_axes):
    return pltpu.CompilerParams(
        dimension_semantics=("arbitrary",) * n_axes, vmem_limit_bytes=VMEM_LIMIT)


def _rms(x, w):
    return x * lax.rsqrt(jnp.mean(x * x, axis=-1, keepdims=True) + EPS) * w


IN_CHUNK = 512
Q_SCALE = HEAD_DIM ** -0.5 * math.log2(math.e)


def _norm_rope(t, w, cos, sin):
    y = _rms(t, w)
    lane = lax.broadcasted_iota(jnp.int32, y.shape, 1)
    partner = jnp.where((lane % 64) < 32,
                        pltpu.roll(y, HEAD_DIM - 32, 1), pltpu.roll(y, 32, 1))
    return y * cos + partner * sin


def _token_table(row_ref, col_ref):
    col = col_ref[...]
    return jnp.concatenate(
        [jnp.broadcast_to(row_ref[g:g + 1, :], col.shape) + col for g in range(row_ref.shape[0])],
        axis=0)


def _inproj_kernel(x_ref, nw_ref, w_ref, cosr_ref, sinr_ref, cosc_ref, sinc_ref, qw_ref, kw_ref,
                   q_ref, k_ref, vt_ref, hp_ref):
    hn = _rms(x_ref[...], nw_ref[...]).astype(BF16)
    cos = _token_table(cosr_ref, cosc_ref)
    sin = _token_table(sinr_ref, sinc_ref)
    n_chunks = IN_DIM // IN_CHUNK
    for c in range(n_chunks):
        acc = jnp.dot(hn, w_ref[:, c * IN_CHUNK:(c + 1) * IN_CHUNK],
                      preferred_element_type=F32)
        if c < 2:
            for h in range(4):
                t = _norm_rope(acc[:, h * 128:(h + 1) * 128], qw_ref[...], cos, sin)
                col = (c * 4 + h) * 128
                q_ref[:, col:col + 128] = (t * Q_SCALE).astype(BF16)
        elif c == 2:
            for h in range(KV_HEADS):
                t = _norm_rope(acc[:, h * 128:(h + 1) * 128], kw_ref[...], cos, sin)
                k_ref[:, h * 128:(h + 1) * 128] = t.astype(BF16)
            for h in range(KV_HEADS):
                v = acc[:, KV_DIM + h * 128:KV_DIM + (h + 1) * 128]
                vt_ref[h, 0] = v.T.astype(BF16)
        else:
            col = (c - 3) * IN_CHUNK
            hp_ref[:, col:col + IN_CHUNK] = acc.astype(BF16)


def _inproj(x, norm_w, w_in, tables, q_w, k_w, tm, seq_len):
    T = x.shape[0]
    n_tab = seq_len // tm
    tr = tm // GRID_W
    grid = (T // tm,)
    full = lambda i: (0, 0)
    row_spec = pl.BlockSpec((tr, HEAD_DIM), lambda i: (i % n_tab, 0))
    col_spec = pl.BlockSpec((GRID_W, HEAD_DIM), full)
    return pl.pallas_call(
        _inproj_kernel,
        grid=grid,
        in_specs=[
            pl.BlockSpec((tm, D_MODEL), lambda i: (i, 0)),
            pl.BlockSpec((1, D_MODEL), full),
            pl.BlockSpec((D_MODEL, IN_DIM), full, pipeline_mode=pl.Buffered(1)),
            row_spec, row_spec, col_spec, col_spec,
            pl.BlockSpec((1, HEAD_DIM), full),
            pl.BlockSpec((1, HEAD_DIM), full),
        ],
        out_specs=[
            pl.BlockSpec((tm, ATTN_DIM), lambda i: (i, 0)),
            pl.BlockSpec((tm, KV_DIM), lambda i: (i, 0)),
            pl.BlockSpec((KV_HEADS, 1, HEAD_DIM, tm), lambda i: (0, i, 0, 0)),
            pl.BlockSpec((tm, HP_DIM), lambda i: (i, 0)),
        ],
        out_shape=[
            jax.ShapeDtypeStruct((T, ATTN_DIM), BF16),
            jax.ShapeDtypeStruct((T, KV_DIM), BF16),
            jax.ShapeDtypeStruct((KV_HEADS, T // tm, HEAD_DIM, tm), BF16),
            jax.ShapeDtypeStruct((T, HP_DIM), BF16),
        ],
        compiler_params=_cparams(1),
        name="inproj",
    )(x, norm_w, w_in, *tables, q_w, k_w)


def _attn_kernel(q_ref, k_ref, vt_ref, km_ref, vtm_ref, o_ref, qt_sc, s0_sc, s1_sc, acc_sc,
                 *, tq, n_chunks):
    heads = range(GROUP)
    cols = [slice(h * tq, (h + 1) * tq) for h in heads]
    qt = q_ref[...].astype(F32).T
    for h in heads:
        qt_sc[:, cols[h]] = qt[h * 128:(h + 1) * 128, :].astype(BF16)

    subs = ATT_CHUNK // VT_CHUNK

    def scores(j, s_sc):
        start = pl.multiple_of(j * ATT_CHUNK, ATT_CHUNK)
        kc = k_ref[pl.ds(start, ATT_CHUNK), :]
        cm = []
        for h in heads:
            s = jnp.dot(kc, qt_sc[:, cols[h]], preferred_element_type=F32)
            s_sc[:, cols[h]] = s
            cm.append(jnp.max(s, axis=0, keepdims=True))
        return tuple(cm)

    def absorb(j, s_sc, m, l, cm):
        vts = [vt_ref[j * subs + r] for r in range(subs)]
        m_out, l_out = [], []
        for h in heads:
            m_new = jnp.maximum(m[h], cm[h])
            alpha = jnp.exp2(m[h] - m_new)
            p = jnp.exp2(s_sc[:, cols[h]] - m_new)
            l_out.append(alpha * l[h] + jnp.sum(p, axis=0, keepdims=True))
            pb = p.astype(BF16)
            pv = sum(jnp.dot(vts[r], pb[r * VT_CHUNK:(r + 1) * VT_CHUNK],
                             preferred_element_type=F32) for r in range(subs))
            acc_sc[:, cols[h]] = alpha * acc_sc[:, cols[h]] + pv
            m_out.append(m_new)
        return tuple(m_out), tuple(l_out)

    m0, l0 = [], []
    for h in heads:
        s = jnp.dot(km_ref[...], qt_sc[:, cols[h]], preferred_element_type=F32)
        row = lax.broadcasted_iota(jnp.int32, s.shape, 0)
        s = jnp.where(row < N_META, s, NEG_BIG)
        mh = jnp.max(s, axis=0, keepdims=True)
        p = jnp.exp2(s - mh)
        m0.append(mh)
        l0.append(jnp.sum(p, axis=0, keepdims=True))
        acc_sc[:, cols[h]] = jnp.dot(vtm_ref[...], p.astype(BF16), preferred_element_type=F32)
    cm0 = scores(0, s0_sc)

    def pair(i, carry):
        m, l, cm = carry
        cm1 = scores(2 * i + 1, s1_sc)
        m, l = absorb(2 * i, s0_sc, m, l, cm)
        cm2 = scores(2 * i + 2, s0_sc)
        m, l = absorb(2 * i + 1, s1_sc, m, l, cm1)
        return m, l, cm2

    m, l, cm = lax.fori_loop(0, n_chunks // 2 - 1, pair, (tuple(m0), tuple(l0), cm0))
    cm1 = scores(n_chunks - 1, s1_sc)
    m, l = absorb(n_chunks - 2, s0_sc, m, l, cm)
    m, l = absorb(n_chunks - 1, s1_sc, m, l, cm1)

    for h in heads:
        o_t = acc_sc[:, cols[h]] * (1.0 / l[h])
        o_ref[:, h * 128:(h + 1) * 128] = o_t.T.astype(BF16)


def _attention(q, k, vt, km, vtm, B, n, tq):
    Bq, nq, _ = q.shape
    n_chunks = n // ATT_CHUNK
    n_vt = n // VT_CHUNK
    assert n_chunks >= 2 and n_chunks % 2 == 0
    k3 = k.reshape(B, n, KV_DIM)
    qb = (lambda b: b) if Bq > 1 else (lambda b: 0)
    return pl.pallas_call(
        functools.partial(_attn_kernel, tq=tq, n_chunks=n_chunks),
        grid=(B, KV_HEADS, nq // tq),
        in_specs=[
            pl.BlockSpec((None, tq, GROUP * HEAD_DIM), lambda b, g, i: (qb(b), i, g)),
            pl.BlockSpec((None, n, HEAD_DIM), lambda b, g, i: (b, 0, g)),
            pl.BlockSpec((None, n_vt, HEAD_DIM, VT_CHUNK), lambda b, g, i: (g, b, 0, 0)),
            pl.BlockSpec((META_PAD, HEAD_DIM), lambda b, g, i: (0, g)),
            pl.BlockSpec((None, None, HEAD_DIM, META_PAD), lambda b, g, i: (g, 0, 0, 0)),
        ],
        out_specs=pl.BlockSpec((None, tq, GROUP * HEAD_DIM), lambda b, g, i: (b, i, g)),
        out_shape=jax.ShapeDtypeStruct((B, nq, ATTN_DIM), BF16),
        scratch_shapes=[
            pltpu.VMEM((HEAD_DIM, GROUP * tq), BF16),
            pltpu.VMEM((ATT_CHUNK, GROUP * tq), F32),
            pltpu.VMEM((ATT_CHUNK, GROUP * tq), F32),
            pltpu.VMEM((HEAD_DIM, GROUP * tq), F32),
        ],
        compiler_params=_cparams(3),
        name="attention",
    )(q, k3, vt, km, vtm)


HG_CHUNK = 64
HG_GROUP = 128
HG_BLOCK = 512
HG_HPS = 8
HP_Q, HP_FF, HP_FB, HP_I, HP_G = 0, 8, 16, 24, 32


def _layer0_lower_bound(lb_ref):
    lb = lb_ref[...]
    e = jnp.exp(lb - jnp.max(lb, axis=0, keepdims=True))
    return e[0:1, :] / jnp.sum(e, axis=0, keepdims=True)


def _gla_block(q, z, v, lb, s, reverse, n_valid=None):
    tb = q.shape[0]
    C, G = HG_CHUNK, HG_GROUP
    nc, ng = tb // C, tb // G
    tn = (((0,), (0,)), ((), ()))

    f = lb + (1.0 - lb) * jax.nn.sigmoid(z)
    kk = 1.0 - f
    g = jnp.log(f)
    if n_valid is not None:
        valid = lax.broadcasted_iota(jnp.int32, g.shape, 0) < n_valid
        g = jnp.where(valid, g, 0.0)
        kk = jnp.where(valid, kk, 0.0)

    ri = lax.broadcasted_iota(jnp.int32, (G, G), 0)
    ci = lax.broadcasted_iota(jnp.int32, (G, G), 1)
    causal = (ri <= ci) if reverse else (ri >= ci)
    keep = jnp.logical_and(ri // C == ci // C, causal)
    tri = jnp.where(keep, 1.0, 0.0).astype(BF16)

    g_hi = g.astype(BF16)
    g_lo = (g - g_hi.astype(F32)).astype(BF16)
    ghl = jnp.concatenate([g_hi, g_lo], axis=1)
    b_parts = []
    for gi in range(ng):
        part = jnp.dot(tri, ghl[gi * G:(gi + 1) * G], preferred_element_type=F32)
        b_parts.append(part[:, :HEAD_DIM] + part[:, HEAD_DIM:])
    b = jnp.concatenate(b_parts, axis=0) if ng > 1 else b_parts[0]

    def chunk_rows(idx):
        return [b[c * C + idx:c * C + idx + 1, :] for c in range(nc)]

    def spread(rows):
        return jnp.concatenate([jnp.broadcast_to(x, (C, HEAD_DIM)) for x in rows], axis=0)

    r_rows = chunk_rows(C // 2)
    end_rows = chunk_rows(0 if reverse else C - 1)
    r_all = spread(r_rows)
    q_t = q * (HEAD_DIM ** -0.5) * jnp.exp(b - r_all)
    k_t = kk * jnp.exp(r_all - b)
    q_e = (q_t * spread([jnp.exp(r) for r in r_rows])).astype(BF16)
    k_e = (k_t * spread([jnp.exp(e - r) for e, r in zip(end_rows, r_rows)])).astype(BF16)
    q_tb = q_t.astype(BF16)
    k_tb = k_t.astype(BF16)
    vb = v.astype(BF16)

    o_intra = []
    for gi in range(ng):
        sl = slice(gi * G, (gi + 1) * G)
        a = jnp.dot(q_tb[sl], k_t[sl].T.astype(BF16), preferred_element_type=F32)
        a = jnp.where(keep, a, 0.0).astype(BF16)
        o_intra.append(jnp.dot(a, vb[sl], preferred_element_type=F32))

    ends = jnp.concatenate(end_rows + [jnp.zeros((HEAD_DIM - nc, HEAD_DIM), F32)], axis=0)
    decay_t = jnp.exp(ends).T

    o_inter = [None] * nc
    for c in (reversed(range(nc)) if reverse else range(nc)):
        sl = slice(c * C, (c + 1) * C)
        o_inter[c] = jnp.dot(q_e[sl], s.astype(BF16), preferred_element_type=F32)
        kv = lax.dot_general(k_e[sl], vb[sl], tn, preferred_element_type=F32)
        s = s * decay_t[:, c:c + 1] + kv
    o = jnp.concatenate(o_intra, axis=0) + jnp.concatenate(o_inter, axis=0)
    return o, s


def _head_cols(h):
    return slice(h * HEAD_DIM, (h + 1) * HEAD_DIM)


def _hgrn_fwd_kernel(q_ref, z_ref, v_ref, qm_ref, zm_ref, vm_ref, lb_ref,
                     o_ref, om_ref, s_sc):
    i = pl.program_id(2)
    lb = _layer0_lower_bound(lb_ref)

    @pl.when(i == 0)
    def _():
        for h in range(HG_HPS):
            cs = _head_cols(h)
            o_m, s1 = _gla_block(qm_ref[:, cs].astype(F32), zm_ref[:, cs].astype(F32),
                                 vm_ref[:, cs].astype(F32), lb[:, cs],
                                 jnp.zeros((HEAD_DIM, HEAD_DIM), F32), False, n_valid=N_META)
            om_ref[:, cs] = o_m
            s_sc[h] = s1

    for h in range(HG_HPS):
        cs = _head_cols(h)
        o, s_t = _gla_block(q_ref[:, cs].astype(F32), z_ref[:, cs].astype(F32),
                            v_ref[:, cs].astype(F32), lb[:, cs], s_sc[h], False)
        o_ref[:, cs] = o
        s_sc[h] = s_t


def _hgrn_bwd_kernel(q_ref, z_ref, v_ref, gate_ref, of_ref,
                     qm_ref, zm_ref, vm_ref, gatem_ref, ofm_ref, lb_ref, nw_ref,
                     out_ref, outm_ref, s_sc):
    i = pl.program_id(2)
    last = pl.num_programs(2) - 1
    lb = _layer0_lower_bound(lb_ref)
    nw = nw_ref[...]

    def finish(o_f, o_b, gate, w):
        return (_rms(o_f + o_b, w) * jax.nn.silu(gate.astype(F32))).astype(BF16)

    @pl.when(i == 0)
    def _():
        s_sc[...] = jnp.zeros_like(s_sc)
        outm_ref[...] = jnp.zeros_like(outm_ref)

    for h in range(HG_HPS):
        cs = _head_cols(h)
        o, s_t = _gla_block(q_ref[:, cs].astype(F32), z_ref[:, cs].astype(F32),
                            v_ref[:, cs].astype(F32), lb[:, cs], s_sc[h], True)
        out_ref[:, cs] = finish(of_ref[:, cs], o, gate_ref[:, cs], nw[:, cs])
        s_sc[h] = s_t

    @pl.when(i == last)
    def _():
        for h in range(HG_HPS):
            cs = _head_cols(h)
            o_m, _ = _gla_block(qm_ref[:, cs].astype(F32), zm_ref[:, cs].astype(F32),
                                vm_ref[:, cs].astype(F32), lb[:, cs], s_sc[h], True,
                                n_valid=N_META)
            outm_ref[:, cs] = finish(ofm_ref[:, cs], o_m, gatem_ref[:, cs], nw[:, cs])


def _hgrn(hp, hpm, lb_fwd, lb_bwd, hg_norm_w, B, n):
    tb = HG_BLOCK
    nb = n // tb
    W = HG_HPS * HEAD_DIM
    grid = (B, HG_HEADS // HG_HPS, nb)

    def tok(off, rev):
        cb = off // HG_HPS
        if rev:
            return pl.BlockSpec((tb, W), lambda b, h, i: (b * nb + nb - 1 - i, cb + h))
        return pl.BlockSpec((tb, W), lambda b, h, i: (b * nb + i, cb + h))

    def met(off):
        cb = off // HG_HPS
        return pl.BlockSpec((META_PAD, W), lambda b, h, i: (0, cb + h))

    lb_spec = pl.BlockSpec((lb_fwd.shape[0], W), lambda b, h, i: (0, h))
    meta_out = pl.BlockSpec((None, META_PAD, W), lambda b, h, i: (b, 0, h))
    state = pltpu.VMEM((HG_HPS, HEAD_DIM, HEAD_DIM), F32)
    o_f, o_fm = pl.pallas_call(
        _hgrn_fwd_kernel,
        grid=grid,
        in_specs=[tok(HP_Q, False), tok(HP_FF, False), tok(HP_I, False),
                  met(HP_Q), met(HP_FF), met(HP_I), lb_spec],
        out_specs=[pl.BlockSpec((tb, W), lambda b, h, i: (b * nb + i, h)), meta_out],
        out_shape=[
            jax.ShapeDtypeStruct((B * n, HG_DIM), F32),
            jax.ShapeDtypeStruct((B, META_PAD, HG_DIM), F32),
        ],
        scratch_shapes=[state],
        compiler_params=_cparams(3),
        name="hgrn_fwd",
    )(hp, hp, hp, hpm, hpm, hpm, lb_fwd)

    out, out_m = pl.pallas_call(
        _hgrn_bwd_kernel,
        grid=grid,
        in_specs=[tok(HP_Q, True), tok(HP_FB, True), tok(HP_I, True), tok(HP_G, True),
                  pl.BlockSpec((tb, W), lambda b, h, i: (b * nb + nb - 1 - i, h)),
                  met(HP_Q), met(HP_FB), met(HP_I), met(HP_G), meta_out,
                  lb_spec,
                  pl.BlockSpec((1, W), lambda b, h, i: (0, h))],
        out_specs=[pl.BlockSpec((tb, W), lambda b, h, i: (b * nb + nb - 1 - i, h)), meta_out],
        out_shape=[
            jax.ShapeDtypeStruct((B * n, HG_DIM), BF16),
            jax.ShapeDtypeStruct((B, META_PAD, HG_DIM), BF16),
        ],
        scratch_shapes=[state],
        compiler_params=_cparams(3),
        name="hgrn_bwd",
    )(hp, hp, hp, hp, o_f, hpm, hpm, hpm, hpm, o_fm, lb_bwd, hg_norm_w)
    return out, out_m


def _outproj_kernel(x_ref, a_ref, g_ref, wa_ref, wg_ref, o_ref):
    o_ref[...] = (x_ref[...]
                  + jnp.dot(a_ref[...], wa_ref[...], preferred_element_type=F32)
                  + jnp.dot(g_ref[...], wg_ref[...], preferred_element_type=F32))


def _outproj(x, attn, hg, w_out, tm):
    T = x.shape[0]
    return pl.pallas_call(
        _outproj_kernel,
        grid=(T // tm,),
        in_specs=[
            pl.BlockSpec((tm, D_MODEL), lambda i: (i, 0)),
            pl.BlockSpec((tm, ATTN_DIM), lambda i: (i, 0)),
            pl.BlockSpec((tm, HG_DIM), lambda i: (i, 0)),
            pl.BlockSpec((ATTN_DIM, D_MODEL), lambda i: (0, 0), pipeline_mode=pl.Buffered(1)),
            pl.BlockSpec((HG_DIM, D_MODEL), lambda i: (1, 0), pipeline_mode=pl.Buffered(1)),
        ],
        out_specs=pl.BlockSpec((tm, D_MODEL), lambda i: (i, 0)),
        out_shape=jax.ShapeDtypeStruct((T, D_MODEL), F32),
        compiler_params=_cparams(1),
        name="outproj",
    )(x, attn, hg, w_out, w_out)


FFN_TF = 512
HALO = 16


def _ffn_kernel(h_ref, hprev_ref, hnext_ref, hmeta_ref, nw_ref, wv_ref, wg_ref,
                cwv_ref, cwg_ref, cbv_ref, cbg_ref, wd_ref, o_ref, hn_sc, *, tm):
    i = pl.program_id(1)
    f = pl.program_id(2)
    last_i = pl.num_programs(1) - 1

    @pl.when(f == 0)
    def _():
        nw = nw_ref[...]
        h = h_ref[...]
        prev = jnp.where(i == 0, hmeta_ref[...], hprev_ref[...])
        hn_sc[0:HALO, :] = _rms(prev, nw).astype(BF16)
        hn_sc[HALO:HALO + tm, :] = _rms(h, nw).astype(BF16)
        nxt = jnp.where(i == last_i, 0.0, _rms(hnext_ref[...], nw))
        hn_sc[HALO + tm:2 * HALO + tm, :] = nxt.astype(BF16)
        o_ref[...] = h

    hn = hn_sc[...]

    def conv(w_ref, cw_ref, cb_ref):
        u = jnp.dot(hn, w_ref[...], preferred_element_type=F32)
        cw = cw_ref[...]
        return (u[HALO - 1:HALO - 1 + tm] * cw[0:1] + u[HALO:HALO + tm] * cw[1:2]
                + u[HALO + 1:HALO + 1 + tm] * cw[2:3] + cb_ref[...])

    val = conv(wv_ref, cwv_ref, cbv_ref)
    gate = conv(wg_ref, cwg_ref, cbg_ref)
    act = (jax.nn.silu(gate) * val).astype(BF16)
    o_ref[...] += jnp.dot(act, wd_ref[...], preferred_element_type=F32)


def _ffn(h1, h1m, norm_w, w_up, conv_w, conv_b, w_down, B, n, tm):
    tf = FFN_TF
    nf = D_FF // tf
    ni = n // tm
    rb = tm // HALO
    n_halo = n // HALO
    return pl.pallas_call(
        functools.partial(_ffn_kernel, tm=tm),
        grid=(B, ni, nf),
        in_specs=[
            pl.BlockSpec((None, tm, D_MODEL), lambda b, i, f: (b, i, 0)),
            pl.BlockSpec((None, HALO, D_MODEL), lambda b, i, f: (b, jnp.maximum(i * rb - 1, 0), 0)),
            pl.BlockSpec((None, HALO, D_MODEL),
                         lambda b, i, f: (b, jnp.minimum((i + 1) * rb, n_halo - 1), 0)),
            pl.BlockSpec((None, N_META, D_MODEL), lambda b, i, f: (b, 0, 0)),
            pl.BlockSpec((1, D_MODEL), lambda b, i, f: (0, 0)),
            pl.BlockSpec((D_MODEL, tf), lambda b, i, f: (0, f)),
            pl.BlockSpec((D_MODEL, tf), lambda b, i, f: (0, nf + f)),
            pl.BlockSpec((3, tf), lambda b, i, f: (0, f)),
            pl.BlockSpec((3, tf), lambda b, i, f: (0, nf + f)),
            pl.BlockSpec((1, tf), lambda b, i, f: (0, f)),
            pl.BlockSpec((1, tf), lambda b, i, f: (0, nf + f)),
            pl.BlockSpec((tf, D_MODEL), lambda b, i, f: (f, 0)),
        ],
        out_specs=pl.BlockSpec((None, tm, D_MODEL), lambda b, i, f: (b, i, 0)),
        out_shape=jax.ShapeDtypeStruct((B, n, D_MODEL), F32),
        scratch_shapes=[pltpu.VMEM((tm + 2 * HALO, D_MODEL), BF16)],
        compiler_params=_cparams(3),
        name="ffn",
    )(h1, h1, h1, h1m, norm_w, w_up, w_up, conv_w, conv_w, conv_b, conv_b, w_down)


def _rope_tables(n):
    n_freq = HEAD_DIM // 4
    inv_freq = jnp.power(ROPE_THETA, -jnp.arange(n_freq, dtype=F32) / n_freq)
    ang_r = jnp.arange(n // GRID_W, dtype=F32)[:, None] * inv_freq
    ang_c = jnp.arange(GRID_W, dtype=F32)[:, None] * inv_freq
    zr = jnp.zeros((n // GRID_W, HEAD_DIM // 2), F32)
    zc = jnp.zeros((GRID_W, HEAD_DIM // 2), F32)
    cos_row = jnp.concatenate([jnp.cos(ang_r), jnp.cos(ang_r), zr], axis=-1)
    sin_row = jnp.concatenate([-jnp.sin(ang_r), jnp.sin(ang_r), zr], axis=-1)
    cos_col = jnp.concatenate([zc, jnp.cos(ang_c), jnp.cos(ang_c)], axis=-1)
    sin_col = jnp.concatenate([zc, -jnp.sin(ang_c), jnp.sin(ang_c)], axis=-1)
    return cos_row, sin_row, cos_col, sin_col


def _identity_rope_tables(n):
    half = HEAD_DIM // 2
    ones_lo = jnp.concatenate([jnp.ones((1, half), F32), jnp.zeros((1, half), F32)], axis=-1)
    ones_hi = jnp.concatenate([jnp.zeros((1, half), F32), jnp.ones((1, half), F32)], axis=-1)
    return (jnp.tile(ones_lo, (n // GRID_W, 1)), jnp.zeros((n // GRID_W, HEAD_DIM), F32),
            jnp.tile(ones_hi, (GRID_W, 1)), jnp.zeros((GRID_W, HEAD_DIM), F32))


def _trunk(x, meta, mp, wts, tables, tiles):
    (w_in, w_out, norm_mix_w, q_w, k_w, hg_w, lb_fwd, lb_bwd,
     norm_ffn_w, w_up, conv_w, conv_b, w_down) = wts
    q_m, k_m, vt_m, hp_m = mp
    B, n, _ = x.shape
    T = B * n
    xf = x.reshape(T, D_MODEL)

    q, k, vt, hp = _inproj(xf, norm_mix_w, w_in, tables, q_w, k_w, VT_CHUNK, n)
    attn = _attention(q.reshape(B, n, ATTN_DIM), k, vt, k_m, vt_m, B, n, tiles["tq"])
    attn_m = _attention(q_m.reshape(1, META_PAD, ATTN_DIM), k, vt, k_m, vt_m, B, n, META_PAD)
    hg, hg_m = _hgrn(hp, hp_m, lb_fwd, lb_bwd, hg_w, B, n)

    h1 = _outproj(xf, attn.reshape(T, ATTN_DIM), hg, w_out, tiles["tm_out"])
    x_m = jnp.broadcast_to(meta[None], (B, N_META, D_MODEL)).reshape(B * N_META, D_MODEL)
    h1_m = _outproj(x_m, attn_m[:, :N_META].reshape(B * N_META, ATTN_DIM),
                    hg_m[:, :N_META].reshape(B * N_META, HG_DIM), w_out, B * N_META)

    return _ffn(h1.reshape(B, n, D_MODEL), h1_m.reshape(B, N_META, D_MODEL),
                norm_ffn_w, w_up, conv_w, conv_b, w_down, B, n, tiles["tm_ffn"])


def _forward(x_prompt, x_sample, meta_tokens, w_in, w_out, norm_mix_w, q_norm_w, k_norm_w,
             hg_norm_w, lb_fwd, lb_bwd, norm_ffn_w, w_up, conv_w, conv_b, w_down, tiles):
    wts = (w_in[0].astype(BF16), w_out[0].astype(BF16), norm_mix_w[0][None], q_norm_w[0][None],
           k_norm_w[0][None], hg_norm_w[0][None], lb_fwd, lb_bwd, norm_ffn_w[0][None],
           w_up[0].astype(BF16), conv_w[0], conv_b[0][None], w_down[0].astype(BF16))
    tables = _rope_tables(max(x_prompt.shape[1], x_sample.shape[1]))

    meta_pad = jnp.zeros((META_PAD, D_MODEL), F32).at[:N_META].set(meta_tokens)
    mp = _inproj(meta_pad, wts[2], wts[0], _identity_rope_tables(META_PAD), wts[3], wts[4],
                 META_PAD, META_PAD)

    y_p = _trunk(x_prompt, meta_tokens, mp, wts, tables, tiles)
    y_s = _trunk(x_sample, meta_tokens, mp, wts, tables, tiles)
    return (y_p, y_s)


TILES = {"tq": 1024, "tm_out": 512, "tm_ffn": 1024}


def kernel(x_prompt, x_sample, meta_tokens, w_in, w_out, norm_mix_w, q_norm_w, k_norm_w,
           hg_norm_w, lb_fwd, lb_bwd, norm_ffn_w, w_up, conv_w, conv_b, w_down):
    return _forward(x_prompt, x_sample, meta_tokens, w_in, w_out, norm_mix_w, q_norm_w,
                    k_norm_w, hg_norm_w, lb_fwd, lb_bwd, norm_ffn_w, w_up, conv_w, conv_b,
                    w_down, TILES)
```

```python
import functools
import math

import jax
import jax.numpy as jnp
from jax import lax
from jax.experimental import pallas as pl
from jax.experimental.pallas import tpu as pltpu

D_MODEL = 2048
N_META = 16
GRID_W = 64
HEAD_DIM = 128
ATTN_HEADS = 8
KV_HEADS = 2
GROUP = ATTN_HEADS // KV_HEADS
ATTN_DIM = ATTN_HEADS * HEAD_DIM
KV_DIM = KV_HEADS * HEAD_DIM
HG_HEADS = 8
HG_DIM = HG_HEADS * HEAD_DIM
HP_DIM = 5 * HG_DIM
IN_DIM = ATTN_DIM + 2 * KV_DIM + HP_DIM
D_FF = 5632
ROPE_THETA = 10000.0
EPS = 1e-6

META_PAD = 128
VT_CHUNK = 512
ATT_CHUNK = 512
NEG_BIG = -1e30
VMEM_LIMIT = 56 * 1024 * 1024

F32 = jnp.float32
BF16 = jnp.bfloat16


def _cparams(n_axes):
    return pltpu.CompilerParams(
        dimension_semantics=("arbitrary",) * n_axes, vmem_limit_bytes=VMEM_LIMIT)


def _rms(x, w):
    return x * lax.rsqrt(jnp.mean(x * x, axis=-1, keepdims=True) + EPS) * w


IN_CHUNK = 512
Q_SCALE = HEAD_DIM ** -0.5 * math.log2(math.e)


def _norm_rope(t, w, cos, sin):
    y = _rms(t, w)
    lane = lax.broadcasted_iota(jnp.int32, y.shape, 1)
    partner = jnp.where((lane % 64) < 32,
                        pltpu.roll(y, HEAD_DIM - 32, 1), pltpu.roll(y, 32, 1))
    return y * cos + partner * sin


def _token_table(row_ref, col_ref):
    col = col_ref[...]
    return jnp.concatenate(
        [jnp.broadcast_to(row_ref[g:g + 1, :], col.shape) + col for g in range(row_ref.shape[0])],
        axis=0)


def _inproj_kernel(x_ref, nw_ref, w_ref, cosr_ref, sinr_ref, cosc_ref, sinc_ref, qw_ref, kw_ref,
                   q_ref, k_ref, vt_ref, hp_ref):
    hn = _rms(x_ref[...], nw_ref[...]).astype(BF16)
    cos = _token_table(cosr_ref, cosc_ref)
    sin = _token_table(sinr_ref, sinc_ref)
    n_chunks = IN_DIM // IN_CHUNK
    for c in range(n_chunks):
        acc = jnp.dot(hn, w_ref[:, c * IN_CHUNK:(c + 1) * IN_CHUNK],
                      preferred_element_type=F32)
        if c < 2:
            for h in range(4):
                t = _norm_rope(acc[:, h * 128:(h + 1) * 128], qw_ref[...], cos, sin)
                col = (c * 4 + h) * 128
                q_ref[:, col:col + 128] = (t * Q_SCALE).astype(BF16)
        elif c == 2:
            for h in range(KV_HEADS):
                t = _norm_rope(acc[:, h * 128:(h + 1) * 128], kw_ref[...], cos, sin)
                k_ref[:, h * 128:(h + 1) * 128] = t.astype(BF16)
            for h in range(KV_HEADS):
                v = acc[:, KV_DIM + h * 128:KV_DIM + (h + 1) * 128]
                vt_ref[h, 0] = v.T.astype(BF16)
        else:
            col = (c - 3) * IN_CHUNK
            hp_ref[:, col:col + IN_CHUNK] = acc.astype(BF16)


def _inproj(x, norm_w, w_in, tables, q_w, k_w, tm, seq_len):
    T = x.shape[0]
    n_tab = seq_len // tm
    tr = tm // GRID_W
    grid = (T // tm,)
    full = lambda i: (0, 0)
    row_spec = pl.BlockSpec((tr, HEAD_DIM), lambda i: (i % n_tab, 0))
    col_spec = pl.BlockSpec((GRID_W, HEAD_DIM), full)
    return pl.pallas_call(
        _inproj_kernel,
        grid=grid,
        in_specs=[
            pl.BlockSpec((tm, D_MODEL), lambda i: (i, 0)),
            pl.BlockSpec((1, D_MODEL), full),
            pl.BlockSpec((D_MODEL, IN_DIM), full, pipeline_mode=pl.Buffered(1)),
            row_spec, row_spec, col_spec, col_spec,
            pl.BlockSpec((1, HEAD_DIM), full),
            pl.BlockSpec((1, HEAD_DIM), full),
        ],
        out_specs=[
            pl.BlockSpec((tm, ATTN_DIM), lambda i: (i, 0)),
            pl.BlockSpec((tm, KV_DIM), lambda i: (i, 0)),
            pl.BlockSpec((KV_HEADS, 1, HEAD_DIM, tm), lambda i: (0, i, 0, 0)),
            pl.BlockSpec((tm, HP_DIM), lambda i: (i, 0)),
        ],
        out_shape=[
            jax.ShapeDtypeStruct((T, ATTN_DIM), BF16),
            jax.ShapeDtypeStruct((T, KV_DIM), BF16),
            jax.ShapeDtypeStruct((KV_HEADS, T // tm, HEAD_DIM, tm), BF16),
            jax.ShapeDtypeStruct((T, HP_DIM), BF16),
        ],
        compiler_params=_cparams(1),
        name="inproj",
    )(x, norm_w, w_in, *tables, q_w, k_w)


def _attn_kernel(q_ref, k_ref, vt_ref, km_ref, vtm_ref, o_ref, qt_sc, s0_sc, s1_sc, acc_sc,
                 *, tq, n_chunks):
    heads = range(GROUP)
    cols = [slice(h * tq, (h + 1) * tq) for h in heads]
    qt = q_ref[...].astype(F32).T
    for h in heads:
        qt_sc[:, cols[h]] = qt[h * 128:(h + 1) * 128, :].astype(BF16)

    subs = ATT_CHUNK // VT_CHUNK

    def scores(j, s_sc):
        start = pl.multiple_of(j * ATT_CHUNK, ATT_CHUNK)
        kc = k_ref[pl.ds(start, ATT_CHUNK), :]
        cm = []
        for h in heads:
            s = jnp.dot(kc, qt_sc[:, cols[h]], preferred_element_type=F32)
            s_sc[:, cols[h]] = s
            cm.append(jnp.max(s, axis=0, keepdims=True))
        return tuple(cm)

    def absorb(j, s_sc, m, l, cm):
        vts = [vt_ref[j * subs + r] for r in range(subs)]
        m_out, l_out = [], []
        for h in heads:
            m_new = jnp.maximum(m[h], cm[h])
            alpha = jnp.exp2(m[h] - m_new)
            p = jnp.exp2(s_sc[:, cols[h]] - m_new)
            l_out.append(alpha * l[h] + jnp.sum(p, axis=0, keepdims=True))
            pb = p.astype(BF16)
            pv = sum(jnp.dot(vts[r], pb[r * VT_CHUNK:(r + 1) * VT_CHUNK],
                             preferred_element_type=F32) for r in range(subs))
            acc_sc[:, cols[h]] = alpha * acc_sc[:, cols[h]] + pv
            m_out.append(m_new)
        return tuple(m_out), tuple(l_out)

    m0, l0 = [], []
    for h in heads:
        s = jnp.dot(km_ref[...], qt_sc[:, cols[h]], preferred_element_type=F32)
        row = lax.broadcasted_iota(jnp.int32, s.shape, 0)
        s = jnp.where(row < N_META, s, NEG_BIG)
        mh = jnp.max(s, axis=0, keepdims=True)
        p = jnp.exp2(s - mh)
        m0.append(mh)
        l0.append(jnp.sum(p, axis=0, keepdims=True))
        acc_sc[:, cols[h]] = jnp.dot(vtm_ref[...], p.astype(BF16), preferred_element_type=F32)
    cm0 = scores(0, s0_sc)

    def pair(i, carry):
        m, l, cm = carry
        cm1 = scores(2 * i + 1, s1_sc)
        m, l = absorb(2 * i, s0_sc, m, l, cm)
        cm2 = scores(2 * i + 2, s0_sc)
        m, l = absorb(2 * i + 1, s1_sc, m, l, cm1)
        return m, l, cm2

    m, l, cm = lax.fori_loop(0, n_chunks // 2 - 1, pair, (tuple(m0), tuple(l0), cm0))
    cm1 = scores(n_chunks - 1, s1_sc)
    m, l = absorb(n_chunks - 2, s0_sc, m, l, cm)
    m, l = absorb(n_chunks - 1, s1_sc, m, l, cm1)

    for h in heads:
        o_t = acc_sc[:, cols[h]] * (1.0 / l[h])
        o_ref[:, h * 128:(h + 1) * 128] = o_t.T.astype(BF16)


def _attention(q, k, vt, km, vtm, B, n, tq):
    Bq, nq, _ = q.shape
    n_chunks = n // ATT_CHUNK
    n_vt = n // VT_CHUNK
    assert n_chunks >= 2 and n_chunks % 2 == 0
    k3 = k.reshape(B, n, KV_DIM)
    qb = (lambda b: b) if Bq > 1 else (lambda b: 0)
    return pl.pallas_call(
        functools.partial(_attn_kernel, tq=tq, n_chunks=n_chunks),
        grid=(B, KV_HEADS, nq // tq),
        in_specs=[
            pl.BlockSpec((None, tq, GROUP * HEAD_DIM), lambda b, g, i: (qb(b), i, g)),
            pl.BlockSpec((None, n, HEAD_DIM), lambda b, g, i: (b, 0, g)),
            pl.BlockSpec((None, n_vt, HEAD_DIM, VT_CHUNK), lambda b, g, i: (g, b, 0, 0)),
            pl.BlockSpec((META_PAD, HEAD_DIM), lambda b, g, i: (0, g)),
            pl.BlockSpec((None, None, HEAD_DIM, META_PAD), lambda b, g, i: (g, 0, 0, 0)),
        ],
        out_specs=pl.BlockSpec((None, tq, GROUP * HEAD_DIM), lambda b, g, i: (b, i, g)),
        out_shape=jax.ShapeDtypeStruct((B, nq, ATTN_DIM), BF16),
        scratch_shapes=[
            pltpu.VMEM((HEAD_DIM, GROUP * tq), BF16),
            pltpu.VMEM((ATT_CHUNK, GROUP * tq), F32),
            pltpu.VMEM((ATT_CHUNK, GROUP * tq), F32),
            pltpu.VMEM((HEAD_DIM, GROUP * tq), F32),
        ],
        compiler_params=_cparams(3),
        name="attention",
    )(q, k3, vt, km, vtm)


HG_CHUNK = 64
HG_GROUP = 128
HG_BLOCK = 1024
HG_HPS = 8
HP_Q, HP_FF, HP_FB, HP_I, HP_G = 0, 8, 16, 24, 32


def _layer0_lower_bound(lb_ref):
    lb = lb_ref[...]
    e = jnp.exp(lb - jnp.max(lb, axis=0, keepdims=True))
    return e[0:1, :] / jnp.sum(e, axis=0, keepdims=True)


def _gla_block(q, z, v, lb, s, reverse, n_valid=None):
    tb = q.shape[0]
    C, G = HG_CHUNK, HG_GROUP
    nc, ng = tb // C, tb // G
    tn = (((0,), (0,)), ((), ()))

    f = lb + (1.0 - lb) * jax.nn.sigmoid(z)
    kk = 1.0 - f
    g = jnp.log(f)
    if n_valid is not None:
        valid = lax.broadcasted_iota(jnp.int32, g.shape, 0) < n_valid
        g = jnp.where(valid, g, 0.0)
        kk = jnp.where(valid, kk, 0.0)

    ri = lax.broadcasted_iota(jnp.int32, (G, G), 0)
    ci = lax.broadcasted_iota(jnp.int32, (G, G), 1)
    causal = (ri <= ci) if reverse else (ri >= ci)
    keep = jnp.logical_and(ri // C == ci // C, causal)
    tri = jnp.where(keep, 1.0, 0.0).astype(BF16)

    g_hi = g.astype(BF16)
    g_lo = (g - g_hi.astype(F32)).astype(BF16)
    ghl = jnp.concatenate([g_hi, g_lo], axis=1)
    b_parts = []
    for gi in range(ng):
        part = jnp.dot(tri, ghl[gi * G:(gi + 1) * G], preferred_element_type=F32)
        b_parts.append(part[:, :HEAD_DIM] + part[:, HEAD_DIM:])
    b = jnp.concatenate(b_parts, axis=0) if ng > 1 else b_parts[0]

    def chunk_rows(idx):
        return [b[c * C + idx:c * C + idx + 1, :] for c in range(nc)]

    def spread(rows):
        return jnp.concatenate([jnp.broadcast_to(x, (C, HEAD_DIM)) for x in rows], axis=0)

    r_rows = chunk_rows(C // 2)
    end_rows = chunk_rows(0 if reverse else C - 1)
    r_all = spread(r_rows)
    q_t = q * (HEAD_DIM ** -0.5) * jnp.exp(b - r_all)
    k_t = kk * jnp.exp(r_all - b)
    q_e = (q_t * spread([jnp.exp(r) for r in r_rows])).astype(BF16)
    k_e = (k_t * spread([jnp.exp(e - r) for e, r in zip(end_rows, r_rows)])).astype(BF16)
    q_tb = q_t.astype(BF16)
    k_tb = k_t.astype(BF16)
    vb = v.astype(BF16)

    o_intra = []
    for gi in range(ng):
        sl = slice(gi * G, (gi + 1) * G)
        a = jnp.dot(q_tb[sl], k_t[sl].T.astype(BF16), preferred_element_type=F32)
        a = jnp.where(keep, a, 0.0).astype(BF16)
        o_intra.append(jnp.dot(a, vb[sl], preferred_element_type=F32))

    ends = jnp.concatenate(end_rows + [jnp.zeros((HEAD_DIM - nc, HEAD_DIM), F32)], axis=0)
    decay_t = jnp.exp(ends).T

    o_inter = [None] * nc
    for c in (reversed(range(nc)) if reverse else range(nc)):
        sl = slice(c * C, (c + 1) * C)
        o_inter[c] = jnp.dot(q_e[sl], s.astype(BF16), preferred_element_type=F32)
        kv = lax.dot_general(k_e[sl], vb[sl], tn, preferred_element_type=F32)
        s = s * decay_t[:, c:c + 1] + kv
    o = jnp.concatenate(o_intra, axis=0) + jnp.concatenate(o_inter, axis=0)
    return o, s


def _head_cols(h):
    return slice(h * HEAD_DIM, (h + 1) * HEAD_DIM)


def _hgrn_fwd_kernel(q_ref, z_ref, v_ref, qm_ref, zm_ref, vm_ref, lb_ref,
                     o_ref, om_ref, s_sc):
    i = pl.program_id(2)
    lb = _layer0_lower_bound(lb_ref)

    @pl.when(i == 0)
    def _():
        for h in range(HG_HPS):
            cs = _head_cols(h)
            o_m, s1 = _gla_block(qm_ref[:, cs].astype(F32), zm_ref[:, cs].astype(F32),
                                 vm_ref[:, cs].astype(F32), lb[:, cs],
                                 jnp.zeros((HEAD_DIM, HEAD_DIM), F32), False, n_valid=N_META)
            om_ref[:, cs] = o_m
            s_sc[h] = s1

    for h in range(HG_HPS):
        cs = _head_cols(h)
        o, s_t = _gla_block(q_ref[:, cs].astype(F32), z_ref[:, cs].astype(F32),
                            v_ref[:, cs].astype(F32), lb[:, cs], s_sc[h], False)
        o_ref[:, cs] = o
        s_sc[h] = s_t


def _hgrn_bwd_kernel(q_ref, z_ref, v_ref, gate_ref, of_ref,
                     qm_ref, zm_ref, vm_ref, gatem_ref, ofm_ref, lb_ref, nw_ref,
                     out_ref, outm_ref, s_sc):
    i = pl.program_id(2)
    last = pl.num_programs(2) - 1
    lb = _layer0_lower_bound(lb_ref)
    nw = nw_ref[...]

    def finish(o_f, o_b, gate, w):
        return (_rms(o_f + o_b, w) * jax.nn.silu(gate.astype(F32))).astype(BF16)

    @pl.when(i == 0)
    def _():
        s_sc[...] = jnp.zeros_like(s_sc)
        outm_ref[...] = jnp.zeros_like(outm_ref)

    for h in range(HG_HPS):
        cs = _head_cols(h)
        o, s_t = _gla_block(q_ref[:, cs].astype(F32), z_ref[:, cs].astype(F32),
                            v_ref[:, cs].astype(F32), lb[:, cs], s_sc[h], True)
        out_ref[:, cs] = finish(of_ref[:, cs], o, gate_ref[:, cs], nw[:, cs])
        s_sc[h] = s_t

    @pl.when(i == last)
    def _():
        for h in range(HG_HPS):
            cs = _head_cols(h)
            o_m, _ = _gla_block(qm_ref[:, cs].astype(F32), zm_ref[:, cs].astype(F32),
                                vm_ref[:, cs].astype(F32), lb[:, cs], s_sc[h], True,
                                n_valid=N_META)
            outm_ref[:, cs] = finish(ofm_ref[:, cs], o_m, gatem_ref[:, cs], nw[:, cs])


def _hgrn(hp, hpm, lb_fwd, lb_bwd, hg_norm_w, B, n):
    tb = HG_BLOCK
    nb = n // tb
    W = HG_HPS * HEAD_DIM
    grid = (B, HG_HEADS // HG_HPS, nb)

    def tok(off, rev):
        cb = off // HG_HPS
        if rev:
            return pl.BlockSpec((tb, W), lambda b, h, i: (b * nb + nb - 1 - i, cb + h))
        return pl.BlockSpec((tb, W), lambda b, h, i: (b * nb + i, cb + h))

    def met(off):
        cb = off // HG_HPS
        return pl.BlockSpec((META_PAD, W), lambda b, h, i: (0, cb + h))

    lb_spec = pl.BlockSpec((lb_fwd.shape[0], W), lambda b, h, i: (0, h))
    meta_out = pl.BlockSpec((None, META_PAD, W), lambda b, h, i: (b, 0, h))
    state = pltpu.VMEM((HG_HPS, HEAD_DIM, HEAD_DIM), F32)
    o_f, o_fm = pl.pallas_call(
        _hgrn_fwd_kernel,
        grid=grid,
        in_specs=[tok(HP_Q, False), tok(HP_FF, False), tok(HP_I, False),
                  met(HP_Q), met(HP_FF), met(HP_I), lb_spec],
        out_specs=[pl.BlockSpec((tb, W), lambda b, h, i: (b * nb + i, h)), meta_out],
        out_shape=[
            jax.ShapeDtypeStruct((B * n, HG_DIM), F32),
            jax.ShapeDtypeStruct((B, META_PAD, HG_DIM), F32),
        ],
        scratch_shapes=[state],
        compiler_params=_cparams(3),
        name="hgrn_fwd",
    )(hp, hp, hp, hpm, hpm, hpm, lb_fwd)

    out, out_m = pl.pallas_call(
        _hgrn_bwd_kernel,
        grid=grid,
        in_specs=[tok(HP_Q, True), tok(HP_FB, True), tok(HP_I, True), tok(HP_G, True),
                  pl.BlockSpec((tb, W), lambda b, h, i: (b * nb + nb - 1 - i, h)),
                  met(HP_Q), met(HP_FB), met(HP_I), met(HP_G), meta_out,
                  lb_spec,
                  pl.BlockSpec((1, W), lambda b, h, i: (0, h))],
        out_specs=[pl.BlockSpec((tb, W), lambda b, h, i: (b * nb + nb - 1 - i, h)), meta_out],
        out_shape=[
            jax.ShapeDtypeStruct((B * n, HG_DIM), BF16),
            jax.ShapeDtypeStruct((B, META_PAD, HG_DIM), BF16),
        ],
        scratch_shapes=[state],
        compiler_params=_cparams(3),
        name="hgrn_bwd",
    )(hp, hp, hp, hp, o_f, hpm, hpm, hpm, hpm, o_fm, lb_bwd, hg_norm_w)
    return out, out_m


def _outproj_kernel(x_ref, a_ref, g_ref, wa_ref, wg_ref, o_ref):
    o_ref[...] = (x_ref[...]
                  + jnp.dot(a_ref[...], wa_ref[...], preferred_element_type=F32)
                  + jnp.dot(g_ref[...], wg_ref[...], preferred_element_type=F32))


def _outproj(x, attn, hg, w_out, tm):
    T = x.shape[0]
    return pl.pallas_call(
        _outproj_kernel,
        grid=(T // tm,),
        in_specs=[
            pl.BlockSpec((tm, D_MODEL), lambda i: (i, 0)),
            pl.BlockSpec((tm, ATTN_DIM), lambda i: (i, 0)),
            pl.BlockSpec((tm, HG_DIM), lambda i: (i, 0)),
            pl.BlockSpec((ATTN_DIM, D_MODEL), lambda i: (0, 0), pipeline_mode=pl.Buffered(1)),
            pl.BlockSpec((HG_DIM, D_MODEL), lambda i: (1, 0), pipeline_mode=pl.Buffered(1)),
        ],
        out_specs=pl.BlockSpec((tm, D_MODEL), lambda i: (i, 0)),
        out_shape=jax.ShapeDtypeStruct((T, D_MODEL), F32),
        compiler_params=_cparams(1),
        name="outproj",
    )(x, attn, hg, w_out, w_out)


FFN_TF = 512
HALO = 16


def _ffn_kernel(h_ref, hprev_ref, hnext_ref, hmeta_ref, nw_ref, wv_ref, wg_ref,
                cwv_ref, cwg_ref, cbv_ref, cbg_ref, wd_ref, o_ref, hn_sc, *, tm):
    i = pl.program_id(1)
    f = pl.program_id(2)
    last_i = pl.num_programs(1) - 1

    @pl.when(f == 0)
    def _():
        nw = nw_ref[...]
        h = h_ref[...]
        prev = jnp.where(i == 0, hmeta_ref[...], hprev_ref[...])
        hn_sc[0:HALO, :] = _rms(prev, nw).astype(BF16)
        hn_sc[HALO:HALO + tm, :] = _rms(h, nw).astype(BF16)
        nxt = jnp.where(i == last_i, 0.0, _rms(hnext_ref[...], nw))
        hn_sc[HALO + tm:2 * HALO + tm, :] = nxt.astype(BF16)
        o_ref[...] = h

    hn = hn_sc[...]

    def conv(w_ref, cw_ref, cb_ref):
        u = jnp.dot(hn, w_ref[...], preferred_element_type=F32)
        cw = cw_ref[...]
        return (u[HALO - 1:HALO - 1 + tm] * cw[0:1] + u[HALO:HALO + tm] * cw[1:2]
                + u[HALO + 1:HALO + 1 + tm] * cw[2:3] + cb_ref[...])

    val = conv(wv_ref, cwv_ref, cbv_ref)
    gate = conv(wg_ref, cwg_ref, cbg_ref)
    act = (jax.nn.silu(gate) * val).astype(BF16)
    o_ref[...] += jnp.dot(act, wd_ref[...], preferred_element_type=F32)


def _ffn(h1, h1m, norm_w, w_up, conv_w, conv_b, w_down, B, n, tm):
    tf = FFN_TF
    nf = D_FF // tf
    ni = n // tm
    rb = tm // HALO
    n_halo = n // HALO
    return pl.pallas_call(
        functools.partial(_ffn_kernel, tm=tm),
        grid=(B, ni, nf),
        in_specs=[
            pl.BlockSpec((None, tm, D_MODEL), lambda b, i, f: (b, i, 0)),
            pl.BlockSpec((None, HALO, D_MODEL), lambda b, i, f: (b, jnp.maximum(i * rb - 1, 0), 0)),
            pl.BlockSpec((None, HALO, D_MODEL),
                         lambda b, i, f: (b, jnp.minimum((i + 1) * rb, n_halo - 1), 0)),
            pl.BlockSpec((None, N_META, D_MODEL), lambda b, i, f: (b, 0, 0)),
            pl.BlockSpec((1, D_MODEL), lambda b, i, f: (0, 0)),
            pl.BlockSpec((D_MODEL, tf), lambda b, i, f: (0, f)),
            pl.BlockSpec((D_MODEL, tf), lambda b, i, f: (0, nf + f)),
            pl.BlockSpec((3, tf), lambda b, i, f: (0, f)),
            pl.BlockSpec((3, tf), lambda b, i, f: (0, nf + f)),
            pl.BlockSpec((1, tf), lambda b, i, f: (0, f)),
            pl.BlockSpec((1, tf), lambda b, i, f: (0, nf + f)),
            pl.BlockSpec((tf, D_MODEL), lambda b, i, f: (f, 0)),
        ],
        out_specs=pl.BlockSpec((None, tm, D_MODEL), lambda b, i, f: (b, i, 0)),
        out_shape=jax.ShapeDtypeStruct((B, n, D_MODEL), F32),
        scratch_shapes=[pltpu.VMEM((tm + 2 * HALO, D_MODEL), BF16)],
        compiler_params=_cparams(3),
        name="ffn",
    )(h1, h1, h1, h1m, norm_w, w_up, w_up, conv_w, conv_w, conv_b, conv_b, w_down)


def _rope_tables(n):
    n_freq = HEAD_DIM // 4
    inv_freq = jnp.power(ROPE_THETA, -jnp.arange(n_freq, dtype=F32) / n_freq)
    ang_r = jnp.arange(n // GRID_W, dtype=F32)[:, None] * inv_freq
    ang_c = jnp.arange(GRID_W, dtype=F32)[:, None] * inv_freq
    zr = jnp.zeros((n // GRID_W, HEAD_DIM // 2), F32)
    zc = jnp.zeros((GRID_W, HEAD_DIM // 2), F32)
    cos_row = jnp.concatenate([jnp.cos(ang_r), jnp.cos(ang_r), zr], axis=-1)
    sin_row = jnp.concatenate([-jnp.sin(ang_r), jnp.sin(ang_r), zr], axis=-1)
    cos_col = jnp.concatenate([zc, jnp.cos(ang_c), jnp.cos(ang_c)], axis=-1)
    sin_col = jnp.concatenate([zc, -jnp.sin(ang_c), jnp.sin(ang_c)], axis=-1)
    return cos_row, sin_row, cos_col, sin_col


def _identity_rope_tables(n):
    half = HEAD_DIM // 2
    ones_lo = jnp.concatenate([jnp.ones((1, half), F32), jnp.zeros((1, half), F32)], axis=-1)
    ones_hi = jnp.concatenate([jnp.zeros((1, half), F32), jnp.ones((1, half), F32)], axis=-1)
    return (jnp.tile(ones_lo, (n // GRID_W, 1)), jnp.zeros((n // GRID_W, HEAD_DIM), F32),
            jnp.tile(ones_hi, (GRID_W, 1)), jnp.zeros((GRID_W, HEAD_DIM), F32))


def _trunk(x, meta, mp, wts, tables, tiles):
    (w_in, w_out, norm_mix_w, q_w, k_w, hg_w, lb_fwd, lb_bwd,
     norm_ffn_w, w_up, conv_w, conv_b, w_down) = wts
    q_m, k_m, vt_m, hp_m = mp
    B, n, _ = x.shape
    T = B * n
    xf = x.reshape(T, D_MODEL)

    q, k, vt, hp = _inproj(xf, norm_mix_w, w_in, tables, q_w, k_w, VT_CHUNK, n)
    attn = _attention(q.reshape(B, n, ATTN_DIM), k, vt, k_m, vt_m, B, n, tiles["tq"])
    attn_m = _attention(q_m.reshape(1, META_PAD, ATTN_DIM), k, vt, k_m, vt_m, B, n, META_PAD)
    hg, hg_m = _hgrn(hp, hp_m, lb_fwd, lb_bwd, hg_w, B, n)

    h1 = _outproj(xf, attn.reshape(T, ATTN_DIM), hg, w_out, tiles["tm_out"])
    x_m = jnp.broadcast_to(meta[None], (B, N_META, D_MODEL)).reshape(B * N_META, D_MODEL)
    h1_m = _outproj(x_m, attn_m[:, :N_META].reshape(B * N_META, ATTN_DIM),
                    hg_m[:, :N_META].reshape(B * N_META, HG_DIM), w_out, B * N_META)

    return _ffn(h1.reshape(B, n, D_MODEL), h1_m.reshape(B, N_META, D_MODEL),
                norm_ffn_w, w_up, conv_w, conv_b, w_down, B, n, tiles["tm_ffn"])


def _forward(x_prompt, x_sample, meta_tokens, w_in, w_out, norm_mix_w, q_norm_w, k_norm_w,
             hg_norm_w, lb_fwd, lb_bwd, norm_ffn_w, w_up, conv_w, conv_b, w_down, tiles):
    wts = (w_in[0].astype(BF16), w_out[0].astype(BF16), norm_mix_w[0][None], q_norm_w[0][None],
           k_norm_w[0][None], hg_norm_w[0][None], lb_fwd, lb_bwd, norm_ffn_w[0][None],
           w_up[0].astype(BF16), conv_w[0], conv_b[0][None], w_down[0].astype(BF16))
    tables = _rope_tables(max(x_prompt.shape[1], x_sample.shape[1]))

    meta_pad = jnp.zeros((META_PAD, D_MODEL), F32).at[:N_META].set(meta_tokens)
    mp = _inproj(meta_pad, wts[2], wts[0], _identity_rope_tables(META_PAD), wts[3], wts[4],
                 META_PAD, META_PAD)

    y_p = _trunk(x_prompt, meta_tokens, mp, wts, tables, tiles)
    y_s = _trunk(x_sample, meta_tokens, mp, wts, tables, tiles)
    return (y_p, y_s)


TILES = {"tq": 1024, "tm_out": 512, "tm_ffn": 1024}


def kernel(x_prompt, x_sample, meta_tokens, w_in, w_out, norm_mix_w, q_norm_w, k_norm_w,
           hg_norm_w, lb_fwd, lb_bwd, norm_ffn_w, w_up, conv_w, conv_b, w_down):
    return _forward(x_prompt, x_sample, meta_tokens, w_in, w_out, norm_mix_w, q_norm_w,
                    k_norm_w, hg_norm_w, lb_fwd, lb_bwd, norm_ffn_w, w_up, conv_w, conv_b,
                    w_down, TILES)
```

```python
import functools
import math

import jax
import jax.numpy as jnp
from jax import lax
from jax.experimental import pallas as pl
from jax.experimental.pallas import tpu as pltpu

D_MODEL = 2048
N_META = 16
GRID_W = 64
HEAD_DIM = 128
ATTN_HEADS = 8
KV_HEADS = 2
GROUP = ATTN_HEADS // KV_HEADS
ATTN_DIM = ATTN_HEADS * HEAD_DIM
KV_DIM = KV_HEADS * HEAD_DIM
HG_HEADS = 8
HG_DIM = HG_HEADS * HEAD_DIM
HP_DIM = 5 * HG_DIM
IN_DIM = ATTN_DIM + 2 * KV_DIM + HP_DIM
D_FF = 5632
ROPE_THETA = 10000.0
EPS = 1e-6

META_PAD = 128
VT_CHUNK = 512
ATT_CHUNK = 512
KEY_SLAB = 256
NEG_BIG = -1e30
VMEM_LIMIT = 56 * 1024 * 1024

F32 = jnp.float32
BF16 = jnp.bfloat16


def _cparams(n_axes):
    return pltpu.CompilerParams(
        dimension_semantics=("arbitrary",) * n_axes, vmem_limit_bytes=VMEM_LIMIT)


def _rms(x, w):
    return x * lax.rsqrt(jnp.mean(x * x, axis=-1, keepdims=True) + EPS) * w


IN_CHUNK = 512
Q_SCALE = HEAD_DIM ** -0.5 * math.log2(math.e)


def _norm_rope(t, w, cos, sin):
    y = _rms(t, w)
    lane = lax.broadcasted_iota(jnp.int32, y.shape, 1)
    partner = jnp.where((lane % 64) < 32,
                        pltpu.roll(y, HEAD_DIM - 32, 1), pltpu.roll(y, 32, 1))
    return y * cos + partner * sin


def _token_table(row_ref, col_ref):
    col = col_ref[...]
    return jnp.concatenate(
        [jnp.broadcast_to(row_ref[g:g + 1, :], col.shape) + col for g in range(row_ref.shape[0])],
        axis=0)


def _inproj_kernel(x_ref, nw_ref, w_ref, cosr_ref, sinr_ref, cosc_ref, sinc_ref, qw_ref, kw_ref,
                   q_ref, k_ref, vt_ref, hp_ref):
    hn = _rms(x_ref[...], nw_ref[...]).astype(BF16)
    cos = _token_table(cosr_ref, cosc_ref)
    sin = _token_table(sinr_ref, sinc_ref)
    n_chunks = IN_DIM // IN_CHUNK
    for c in range(n_chunks):
        acc = jnp.dot(hn, w_ref[:, c * IN_CHUNK:(c + 1) * IN_CHUNK],
                      preferred_element_type=F32)
        if c < 2:
            for h in range(4):
                t = _norm_rope(acc[:, h * 128:(h + 1) * 128], qw_ref[...], cos, sin)
                col = (c * 4 + h) * 128
                q_ref[:, col:col + 128] = (t * Q_SCALE).astype(BF16)
        elif c == 2:
            for h in range(KV_HEADS):
                t = _norm_rope(acc[:, h * 128:(h + 1) * 128], kw_ref[...], cos, sin)
                k_ref[:, h * 128:(h + 1) * 128] = t.astype(BF16)
            for h in range(KV_HEADS):
                v = acc[:, KV_DIM + h * 128:KV_DIM + (h + 1) * 128]
                vt_ref[h, 0] = v.T.astype(BF16)
        else:
            col = (c - 3) * IN_CHUNK
            hp_ref[:, col:col + IN_CHUNK] = acc.astype(BF16)


def _inproj(x, norm_w, w_in, tables, q_w, k_w, tm, seq_len):
    T = x.shape[0]
    n_tab = seq_len // tm
    tr = tm // GRID_W
    grid = (T // tm,)
    full = lambda i: (0, 0)
    row_spec = pl.BlockSpec((tr, HEAD_DIM), lambda i: (i % n_tab, 0))
    col_spec = pl.BlockSpec((GRID_W, HEAD_DIM), full)
    return pl.pallas_call(
        _inproj_kernel,
        grid=grid,
        in_specs=[
            pl.BlockSpec((tm, D_MODEL), lambda i: (i, 0)),
            pl.BlockSpec((1, D_MODEL), full),
            pl.BlockSpec((D_MODEL, IN_DIM), full, pipeline_mode=pl.Buffered(1)),
            row_spec, row_spec, col_spec, col_spec,
            pl.BlockSpec((1, HEAD_DIM), full),
            pl.BlockSpec((1, HEAD_DIM), full),
        ],
        out_specs=[
            pl.BlockSpec((tm, ATTN_DIM), lambda i: (i, 0)),
            pl.BlockSpec((tm, KV_DIM), lambda i: (i, 0)),
            pl.BlockSpec((KV_HEADS, 1, HEAD_DIM, tm), lambda i: (0, i, 0, 0)),
            pl.BlockSpec((tm, HP_DIM), lambda i: (i, 0)),
        ],
        out_shape=[
            jax.ShapeDtypeStruct((T, ATTN_DIM), BF16),
            jax.ShapeDtypeStruct((T, KV_DIM), BF16),
            jax.ShapeDtypeStruct((KV_HEADS, T // tm, HEAD_DIM, tm), BF16),
            jax.ShapeDtypeStruct((T, HP_DIM), BF16),
        ],
        compiler_params=_cparams(1),
        name="inproj",
    )(x, norm_w, w_in, *tables, q_w, k_w)


def _attn_kernel(q_ref, k_ref, vt_ref, km_ref, vtm_ref, o_ref, qt_sc, s0_sc, s1_sc, acc_sc,
                 *, tq, n_chunks):
    heads = range(GROUP)
    cols = [slice(h * tq, (h + 1) * tq) for h in heads]
    qt = q_ref[...].astype(F32).T
    for h in heads:
        qt_sc[:, cols[h]] = qt[h * 128:(h + 1) * 128, :].astype(BF16)

    subs = ATT_CHUNK // VT_CHUNK

    def scores(j, s_sc):
        start = pl.multiple_of(j * ATT_CHUNK, ATT_CHUNK)
        kc = k_ref[pl.ds(start, ATT_CHUNK), :]
        cm = []
        for h in heads:
            s = jnp.dot(kc, qt_sc[:, cols[h]], preferred_element_type=F32)
            s_sc[:, cols[h]] = s
            cm.append(jnp.max(s, axis=0, keepdims=True))
        return tuple(cm)

    def absorb(j, s_sc, m, l, cm):
        vts = [vt_ref[j * subs + r] for r in range(subs)]
        m_out, l_out = [], []
        for h in heads:
            m_new = jnp.maximum(m[h], cm[h])
            alpha = jnp.exp2(m[h] - m_new)
            l_new = alpha * l[h]
            pv = None
            for r in range(subs):
                for k0 in range(0, VT_CHUNK, KEY_SLAB):
                    p = jnp.exp2(s_sc[r * VT_CHUNK + k0:r * VT_CHUNK + k0 + KEY_SLAB, cols[h]]
                                 - m_new)
                    l_new = l_new + jnp.sum(p, axis=0, keepdims=True)
                    d = jnp.dot(vts[r][:, k0:k0 + KEY_SLAB], p.astype(BF16),
                                preferred_element_type=F32)
                    pv = d if pv is None else pv + d
            l_out.append(l_new)
            acc_sc[:, cols[h]] = alpha * acc_sc[:, cols[h]] + pv
            m_out.append(m_new)
        return tuple(m_out), tuple(l_out)

    m0, l0 = [], []
    for h in heads:
        s = jnp.dot(km_ref[...], qt_sc[:, cols[h]], preferred_element_type=F32)
        row = lax.broadcasted_iota(jnp.int32, s.shape, 0)
        s = jnp.where(row < N_META, s, NEG_BIG)
        mh = jnp.max(s, axis=0, keepdims=True)
        p = jnp.exp2(s - mh)
        m0.append(mh)
        l0.append(jnp.sum(p, axis=0, keepdims=True))
        acc_sc[:, cols[h]] = jnp.dot(vtm_ref[...], p.astype(BF16), preferred_element_type=F32)
    cm0 = scores(0, s0_sc)

    def pair(i, carry):
        m, l, cm = carry
        cm1 = scores(2 * i + 1, s1_sc)
        m, l = absorb(2 * i, s0_sc, m, l, cm)
        cm2 = scores(2 * i + 2, s0_sc)
        m, l = absorb(2 * i + 1, s1_sc, m, l, cm1)
        return m, l, cm2

    m, l, cm = lax.fori_loop(0, n_chunks // 2 - 1, pair, (tuple(m0), tuple(l0), cm0))
    cm1 = scores(n_chunks - 1, s1_sc)
    m, l = absorb(n_chunks - 2, s0_sc, m, l, cm)
    m, l = absorb(n_chunks - 1, s1_sc, m, l, cm1)

    for h in heads:
        o_t = acc_sc[:, cols[h]] * (1.0 / l[h])
        o_ref[:, h * 128:(h + 1) * 128] = o_t.T.astype(BF16)


def _attention(q, k, vt, km, vtm, B, n, tq):
    Bq, nq, _ = q.shape
    n_chunks = n // ATT_CHUNK
    n_vt = n // VT_CHUNK
    assert n_chunks >= 2 and n_chunks % 2 == 0
    k3 = k.reshape(B, n, KV_DIM)
    qb = (lambda b: b) if Bq > 1 else (lambda b: 0)
    return pl.pallas_call(
        functools.partial(_attn_kernel, tq=tq, n_chunks=n_chunks),
        grid=(B, KV_HEADS, nq // tq),
        in_specs=[
            pl.BlockSpec((None, tq, GROUP * HEAD_DIM), lambda b, g, i: (qb(b), i, g)),
            pl.BlockSpec((None, n, HEAD_DIM), lambda b, g, i: (b, 0, g)),
            pl.BlockSpec((None, n_vt, HEAD_DIM, VT_CHUNK), lambda b, g, i: (g, b, 0, 0)),
            pl.BlockSpec((META_PAD, HEAD_DIM), lambda b, g, i: (0, g)),
            pl.BlockSpec((None, None, HEAD_DIM, META_PAD), lambda b, g, i: (g, 0, 0, 0)),
        ],
        out_specs=pl.BlockSpec((None, tq, GROUP * HEAD_DIM), lambda b, g, i: (b, i, g)),
        out_shape=jax.ShapeDtypeStruct((B, nq, ATTN_DIM), BF16),
        scratch_shapes=[
            pltpu.VMEM((HEAD_DIM, GROUP * tq), BF16),
            pltpu.VMEM((ATT_CHUNK, GROUP * tq), F32),
            pltpu.VMEM((ATT_CHUNK, GROUP * tq), F32),
            pltpu.VMEM((HEAD_DIM, GROUP * tq), F32),
        ],
        compiler_params=_cparams(3),
        name="attention",
    )(q, k3, vt, km, vtm)


HG_CHUNK = 64
HG_GROUP = 128
HG_BLOCK = 1024
HG_HPS = 8
HP_Q, HP_FF, HP_FB, HP_I, HP_G = 0, 8, 16, 24, 32


def _layer0_lower_bound(lb_ref):
    lb = lb_ref[...]
    e = jnp.exp(lb - jnp.max(lb, axis=0, keepdims=True))
    return e[0:1, :] / jnp.sum(e, axis=0, keepdims=True)


def _gla_block(q, z, v, lb, s, reverse, n_valid=None):
    tb = q.shape[0]
    C, G = HG_CHUNK, HG_GROUP
    nc, ng = tb // C, tb // G
    tn = (((0,), (0,)), ((), ()))

    f = lb + (1.0 - lb) * jax.nn.sigmoid(z)
    kk = 1.0 - f
    g = jnp.log(f)
    if n_valid is not None:
        valid = lax.broadcasted_iota(jnp.int32, g.shape, 0) < n_valid
        g = jnp.where(valid, g, 0.0)
        kk = jnp.where(valid, kk, 0.0)

    ri = lax.broadcasted_iota(jnp.int32, (G, G), 0)
    ci = lax.broadcasted_iota(jnp.int32, (G, G), 1)
    causal = (ri <= ci) if reverse else (ri >= ci)
    keep = jnp.logical_and(ri // C == ci // C, causal)
    tri = jnp.where(keep, 1.0, 0.0).astype(BF16)

    g_hi = g.astype(BF16)
    g_lo = (g - g_hi.astype(F32)).astype(BF16)
    ghl = jnp.concatenate([g_hi, g_lo], axis=1)
    b_parts = []
    for gi in range(ng):
        part = jnp.dot(tri, ghl[gi * G:(gi + 1) * G], preferred_element_type=F32)
        b_parts.append(part[:, :HEAD_DIM] + part[:, HEAD_DIM:])
    b = jnp.concatenate(b_parts, axis=0) if ng > 1 else b_parts[0]

    def chunk_rows(idx):
        return [b[c * C + idx:c * C + idx + 1, :] for c in range(nc)]

    def spread(rows):
        return jnp.concatenate([jnp.broadcast_to(x, (C, HEAD_DIM)) for x in rows], axis=0)

    r_rows = chunk_rows(C // 2)
    end_rows = chunk_rows(0 if reverse else C - 1)
    r_all = spread(r_rows)
    q_t = q * (HEAD_DIM ** -0.5) * jnp.exp(b - r_all)
    k_t = kk * jnp.exp(r_all - b)
    q_e = (q_t * spread([jnp.exp(r) for r in r_rows])).astype(BF16)
    k_e = (k_t * spread([jnp.exp(e - r) for e, r in zip(end_rows, r_rows)])).astype(BF16)
    q_tb = q_t.astype(BF16)
    k_tb = k_t.astype(BF16)
    vb = v.astype(BF16)

    o_intra = []
    for gi in range(ng):
        sl = slice(gi * G, (gi + 1) * G)
        a = jnp.dot(q_tb[sl], k_t[sl].T.astype(BF16), preferred_element_type=F32)
        a = jnp.where(keep, a, 0.0).astype(BF16)
        o_intra.append(jnp.dot(a, vb[sl], preferred_element_type=F32))

    ends = jnp.concatenate(end_rows + [jnp.zeros((HEAD_DIM - nc, HEAD_DIM), F32)], axis=0)
    decay_t = jnp.exp(ends).T

    o_inter = [None] * nc
    for c in (reversed(range(nc)) if reverse else range(nc)):
        sl = slice(c * C, (c + 1) * C)
        o_inter[c] = jnp.dot(q_e[sl], s.astype(BF16), preferred_element_type=F32)
        kv = lax.dot_general(k_e[sl], vb[sl], tn, preferred_element_type=F32)
        s = s * decay_t[:, c:c + 1] + kv
    o = jnp.concatenate(o_intra, axis=0) + jnp.concatenate(o_inter, axis=0)
    return o, s


def _head_cols(h):
    return slice(h * HEAD_DIM, (h + 1) * HEAD_DIM)


def _hgrn_fwd_kernel(q_ref, z_ref, v_ref, qm_ref, zm_ref, vm_ref, lb_ref,
                     o_ref, om_ref, s_sc):
    i = pl.program_id(2)
    lb = _layer0_lower_bound(lb_ref)

    @pl.when(i == 0)
    def _():
        for h in range(HG_HPS):
            cs = _head_cols(h)
            o_m, s1 = _gla_block(qm_ref[:, cs].astype(F32), zm_ref[:, cs].astype(F32),
                                 vm_ref[:, cs].astype(F32), lb[:, cs],
                                 jnp.zeros((HEAD_DIM, HEAD_DIM), F32), False, n_valid=N_META)
            om_ref[:, cs] = o_m
            s_sc[h] = s1

    for h in range(HG_HPS):
        cs = _head_cols(h)
        o, s_t = _gla_block(q_ref[:, cs].astype(F32), z_ref[:, cs].astype(F32),
                            v_ref[:, cs].astype(F32), lb[:, cs], s_sc[h], False)
        o_ref[:, cs] = o
        s_sc[h] = s_t


def _hgrn_bwd_kernel(q_ref, z_ref, v_ref, gate_ref, of_ref,
                     qm_ref, zm_ref, vm_ref, gatem_ref, ofm_ref, lb_ref, nw_ref,
                     out_ref, outm_ref, s_sc):
    i = pl.program_id(2)
    last = pl.num_programs(2) - 1
    lb = _layer0_lower_bound(lb_ref)
    nw = nw_ref[...]

    def finish(o_f, o_b, gate, w):
        return (_rms(o_f + o_b, w) * jax.nn.silu(gate.astype(F32))).astype(BF16)

    @pl.when(i == 0)
    def _():
        s_sc[...] = jnp.zeros_like(s_sc)
        outm_ref[...] = jnp.zeros_like(outm_ref)

    for h in range(HG_HPS):
        cs = _head_cols(h)
        o, s_t = _gla_block(q_ref[:, cs].astype(F32), z_ref[:, cs].astype(F32),
                            v_ref[:, cs].astype(F32), lb[:, cs], s_sc[h], True)
        out_ref[:, cs] = finish(of_ref[:, cs], o, gate_ref[:, cs], nw[:, cs])
        s_sc[h] = s_t

    @pl.when(i == last)
    def _():
        for h in range(HG_HPS):
            cs = _head_cols(h)
            o_m, _ = _gla_block(qm_ref[:, cs].astype(F32), zm_ref[:, cs].astype(F32),
                                vm_ref[:, cs].astype(F32), lb[:, cs], s_sc[h], True,
                                n_valid=N_META)
            outm_ref[:, cs] = finish(ofm_ref[:, cs], o_m, gatem_ref[:, cs], nw[:, cs])


def _hgrn(hp, hpm, lb_fwd, lb_bwd, hg_norm_w, B, n):
    tb = HG_BLOCK
    nb = n // tb
    W = HG_HPS * HEAD_DIM
    grid = (B, HG_HEADS // HG_HPS, nb)

    def tok(off, rev):
        cb = off // HG_HPS
        if rev:
            return pl.BlockSpec((tb, W), lambda b, h, i: (b * nb + nb - 1 - i, cb + h))
        return pl.BlockSpec((tb, W), lambda b, h, i: (b * nb + i, cb + h))

    def met(off):
        cb = off // HG_HPS
        return pl.BlockSpec((META_PAD, W), lambda b, h, i: (0, cb + h))

    lb_spec = pl.BlockSpec((lb_fwd.shape[0], W), lambda b, h, i: (0, h))
    meta_out = pl.BlockSpec((None, META_PAD, W), lambda b, h, i: (b, 0, h))
    state = pltpu.VMEM((HG_HPS, HEAD_DIM, HEAD_DIM), F32)
    o_f, o_fm = pl.pallas_call(
        _hgrn_fwd_kernel,
        grid=grid,
        in_specs=[tok(HP_Q, False), tok(HP_FF, False), tok(HP_I, False),
                  met(HP_Q), met(HP_FF), met(HP_I), lb_spec],
        out_specs=[pl.BlockSpec((tb, W), lambda b, h, i: (b * nb + i, h)), meta_out],
        out_shape=[
            jax.ShapeDtypeStruct((B * n, HG_DIM), F32),
            jax.ShapeDtypeStruct((B, META_PAD, HG_DIM), F32),
        ],
        scratch_shapes=[state],
        compiler_params=_cparams(3),
        name="hgrn_fwd",
    )(hp, hp, hp, hpm, hpm, hpm, lb_fwd)

    out, out_m = pl.pallas_call(
        _hgrn_bwd_kernel,
        grid=grid,
        in_specs=[tok(HP_Q, True), tok(HP_FB, True), tok(HP_I, True), tok(HP_G, True),
                  pl.BlockSpec((tb, W), lambda b, h, i: (b * nb + nb - 1 - i, h)),
                  met(HP_Q), met(HP_FB), met(HP_I), met(HP_G), meta_out,
                  lb_spec,
                  pl.BlockSpec((1, W), lambda b, h, i: (0, h))],
        out_specs=[pl.BlockSpec((tb, W), lambda b, h, i: (b * nb + nb - 1 - i, h)), meta_out],
        out_shape=[
            jax.ShapeDtypeStruct((B * n, HG_DIM), BF16),
            jax.ShapeDtypeStruct((B, META_PAD, HG_DIM), BF16),
        ],
        scratch_shapes=[state],
        compiler_params=_cparams(3),
        name="hgrn_bwd",
    )(hp, hp, hp, hp, o_f, hpm, hpm, hpm, hpm, o_fm, lb_bwd, hg_norm_w)
    return out, out_m


def _outproj_kernel(x_ref, a_ref, g_ref, wa_ref, wg_ref, o_ref):
    o_ref[...] = (x_ref[...]
                  + jnp.dot(a_ref[...], wa_ref[...], preferred_element_type=F32)
                  + jnp.dot(g_ref[...], wg_ref[...], preferred_element_type=F32))


def _outproj(x, attn, hg, w_out, tm):
    T = x.shape[0]
    return pl.pallas_call(
        _outproj_kernel,
        grid=(T // tm,),
        in_specs=[
            pl.BlockSpec((tm, D_MODEL), lambda i: (i, 0)),
            pl.BlockSpec((tm, ATTN_DIM), lambda i: (i, 0)),
            pl.BlockSpec((tm, HG_DIM), lambda i: (i, 0)),
            pl.BlockSpec((ATTN_DIM, D_MODEL), lambda i: (0, 0), pipeline_mode=pl.Buffered(1)),
            pl.BlockSpec((HG_DIM, D_MODEL), lambda i: (1, 0), pipeline_mode=pl.Buffered(1)),
        ],
        out_specs=pl.BlockSpec((tm, D_MODEL), lambda i: (i, 0)),
        out_shape=jax.ShapeDtypeStruct((T, D_MODEL), F32),
        compiler_params=_cparams(1),
        name="outproj",
    )(x, attn, hg, w_out, w_out)


FFN_TF = 512
HALO = 16


def _ffn_kernel(h_ref, hprev_ref, hnext_ref, hmeta_ref, nw_ref, wv_ref, wg_ref,
                cwv_ref, cwg_ref, cbv_ref, cbg_ref, wd_ref, o_ref, hn_sc, *, tm):
    i = pl.program_id(1)
    f = pl.program_id(2)
    last_i = pl.num_programs(1) - 1

    @pl.when(f == 0)
    def _():
        nw = nw_ref[...]
        h = h_ref[...]
        prev = jnp.where(i == 0, hmeta_ref[...], hprev_ref[...])
        hn_sc[0:HALO, :] = _rms(prev, nw).astype(BF16)
        hn_sc[HALO:HALO + tm, :] = _rms(h, nw).astype(BF16)
        nxt = jnp.where(i == last_i, 0.0, _rms(hnext_ref[...], nw))
        hn_sc[HALO + tm:2 * HALO + tm, :] = nxt.astype(BF16)
        o_ref[...] = h

    hn = hn_sc[...]

    def conv(w_ref, cw_ref, cb_ref):
        u = jnp.dot(hn, w_ref[...], preferred_element_type=F32)
        cw = cw_ref[...]
        return (u[HALO - 1:HALO - 1 + tm] * cw[0:1] + u[HALO:HALO + tm] * cw[1:2]
                + u[HALO + 1:HALO + 1 + tm] * cw[2:3] + cb_ref[...])

    val = conv(wv_ref, cwv_ref, cbv_ref)
    gate = conv(wg_ref, cwg_ref, cbg_ref)
    act = (jax.nn.silu(gate) * val).astype(BF16)
    o_ref[...] += jnp.dot(act, wd_ref[...], preferred_element_type=F32)


def _ffn(h1, h1m, norm_w, w_up, conv_w, conv_b, w_down, B, n, tm):
    tf = FFN_TF
    nf = D_FF // tf
    ni = n // tm
    rb = tm // HALO
    n_halo = n // HALO
    return pl.pallas_call(
        functools.partial(_ffn_kernel, tm=tm),
        grid=(B, ni, nf),
        in_specs=[
            pl.BlockSpec((None, tm, D_MODEL), lambda b, i, f: (b, i, 0)),
            pl.BlockSpec((None, HALO, D_MODEL), lambda b, i, f: (b, jnp.maximum(i * rb - 1, 0), 0)),
            pl.BlockSpec((None, HALO, D_MODEL),
                         lambda b, i, f: (b, jnp.minimum((i + 1) * rb, n_halo - 1), 0)),
            pl.BlockSpec((None, N_META, D_MODEL), lambda b, i, f: (b, 0, 0)),
            pl.BlockSpec((1, D_MODEL), lambda b, i, f: (0, 0)),
            pl.BlockSpec((D_MODEL, tf), lambda b, i, f: (0, f)),
            pl.BlockSpec((D_MODEL, tf), lambda b, i, f: (0, nf + f)),
            pl.BlockSpec((3, tf), lambda b, i, f: (0, f)),
            pl.BlockSpec((3, tf), lambda b, i, f: (0, nf + f)),
            pl.BlockSpec((1, tf), lambda b, i, f: (0, f)),
            pl.BlockSpec((1, tf), lambda b, i, f: (0, nf + f)),
            pl.BlockSpec((tf, D_MODEL), lambda b, i, f: (f, 0)),
        ],
        out_specs=pl.BlockSpec((None, tm, D_MODEL), lambda b, i, f: (b, i, 0)),
        out_shape=jax.ShapeDtypeStruct((B, n, D_MODEL), F32),
        scratch_shapes=[pltpu.VMEM((tm + 2 * HALO, D_MODEL), BF16)],
        compiler_params=_cparams(3),
        name="ffn",
    )(h1, h1, h1, h1m, norm_w, w_up, w_up, conv_w, conv_w, conv_b, conv_b, w_down)


def _rope_tables(n):
    n_freq = HEAD_DIM // 4
    inv_freq = jnp.power(ROPE_THETA, -jnp.arange(n_freq, dtype=F32) / n_freq)
    ang_r = jnp.arange(n // GRID_W, dtype=F32)[:, None] * inv_freq
    ang_c = jnp.arange(GRID_W, dtype=F32)[:, None] * inv_freq
    zr = jnp.zeros((n // GRID_W, HEAD_DIM // 2), F32)
    zc = jnp.zeros((GRID_W, HEAD_DIM // 2), F32)
    cos_row = jnp.concatenate([jnp.cos(ang_r), jnp.cos(ang_r), zr], axis=-1)
    sin_row = jnp.concatenate([-jnp.sin(ang_r), jnp.sin(ang_r), zr], axis=-1)
    cos_col = jnp.concatenate([zc, jnp.cos(ang_c), jnp.cos(ang_c)], axis=-1)
    sin_col = jnp.concatenate([zc, -jnp.sin(ang_c), jnp.sin(ang_c)], axis=-1)
    return cos_row, sin_row, cos_col, sin_col


def _identity_rope_tables(n):
    half = HEAD_DIM // 2
    ones_lo = jnp.concatenate([jnp.ones((1, half), F32), jnp.zeros((1, half), F32)], axis=-1)
    ones_hi = jnp.concatenate([jnp.zeros((1, half), F32), jnp.ones((1, half), F32)], axis=-1)
    return (jnp.tile(ones_lo, (n // GRID_W, 1)), jnp.zeros((n // GRID_W, HEAD_DIM), F32),
            jnp.tile(ones_hi, (GRID_W, 1)), jnp.zeros((GRID_W, HEAD_DIM), F32))


def _trunk(x, meta, mp, wts, tables, tiles):
    (w_in, w_out, norm_mix_w, q_w, k_w, hg_w, lb_fwd, lb_bwd,
     norm_ffn_w, w_up, conv_w, conv_b, w_down) = wts
    q_m, k_m, vt_m, hp_m = mp
    B, n, _ = x.shape
    T = B * n
    xf = x.reshape(T, D_MODEL)

    q, k, vt, hp = _inproj(xf, norm_mix_w, w_in, tables, q_w, k_w, VT_CHUNK, n)
    attn = _attention(q.reshape(B, n, ATTN_DIM), k, vt, k_m, vt_m, B, n, tiles["tq"])
    attn_m = _attention(q_m.reshape(1, META_PAD, ATTN_DIM), k, vt, k_m, vt_m, B, n, META_PAD)
    hg, hg_m = _hgrn(hp, hp_m, lb_fwd, lb_bwd, hg_w, B, n)

    h1 = _outproj(xf, attn.reshape(T, ATTN_DIM), hg, w_out, tiles["tm_out"])
    x_m = jnp.broadcast_to(meta[None], (B, N_META, D_MODEL)).reshape(B * N_META, D_MODEL)
    h1_m = _outproj(x_m, attn_m[:, :N_META].reshape(B * N_META, ATTN_DIM),
                    hg_m[:, :N_META].reshape(B * N_META, HG_DIM), w_out, B * N_META)

    return _ffn(h1.reshape(B, n, D_MODEL), h1_m.reshape(B, N_META, D_MODEL),
                norm_ffn_w, w_up, conv_w, conv_b, w_down, B, n, tiles["tm_ffn"])


def _forward(x_prompt, x_sample, meta_tokens, w_in, w_out, norm_mix_w, q_norm_w, k_norm_w,
             hg_norm_w, lb_fwd, lb_bwd, norm_ffn_w, w_up, conv_w, conv_b, w_down, tiles):
    wts = (w_in[0].astype(BF16), w_out[0].astype(BF16), norm_mix_w[0][None], q_norm_w[0][None],
           k_norm_w[0][None], hg_norm_w[0][None], lb_fwd, lb_bwd, norm_ffn_w[0][None],
           w_up[0].astype(BF16), conv_w[0], conv_b[0][None], w_down[0].astype(BF16))
    tables = _rope_tables(max(x_prompt.shape[1], x_sample.shape[1]))

    meta_pad = jnp.zeros((META_PAD, D_MODEL), F32).at[:N_META].set(meta_tokens)
    mp = _inproj(meta_pad, wts[2], wts[0], _identity_rope_tables(META_PAD), wts[3], wts[4],
                 META_PAD, META_PAD)

    y_p = _trunk(x_prompt, meta_tokens, mp, wts, tables, tiles)
    y_s = _trunk(x_sample, meta_tokens, mp, wts, tables, tiles)
    return (y_p, y_s)


TILES = {"tq": 1024, "tm_out": 512, "tm_ffn": 1024}


def kernel(x_prompt, x_sample, meta_tokens, w_in, w_out, norm_mix_w, q_norm_w, k_norm_w,
           hg_norm_w, lb_fwd, lb_bwd, norm_ffn_w, w_up, conv_w, conv_b, w_down):
    return _forward(x_prompt, x_sample, meta_tokens, w_in, w_out, norm_mix_w, q_norm_w,
                    k_norm_w, hg_norm_w, lb_fwd, lb_bwd, norm_ffn_w, w_up, conv_w, conv_b,
                    w_down, TILES)
```
